```python
import jax, jax.numpy as jnp
from jax import lax
import numpy as np

D_MODEL = 1024
BATCH = 32
SEQ = 2048
DEPTH = 4

N_MIXERS = 3
HEAD_DIM = 64
N_HEADS = D_MODEL // HEAD_DIM
MIX_WIDTH = N_HEADS * HEAD_DIM
Q_BLOCK = 128
ROPE_THETA = 10000.0
NEG_INF = -1e30
PLE_DIM = 256
MAX_POS_OFFSET = 4096
DEEPNORM_ALPHA = (2 * DEPTH) ** 0.25
DEEPNORM_BETA = (8 * DEPTH) ** -0.25
LN_EPS = 1e-5
RMS_EPS = 1e-6

SWA_WINDOW = 128
SWA_KV_HEADS = 4
SWA_GROUP = N_HEADS // SWA_KV_HEADS
SWA_KV = SWA_KV_HEADS * HEAD_DIM
SWA_IN = MIX_WIDTH + 2 * SWA_KV + MIX_WIDTH

MLA_Q_LORA = 384
MLA_KV_LORA = 256
MLA_NOPE = 64
MLA_ROPE = 32
MLA_V = 64
MLA_IN = MLA_Q_LORA + MLA_KV_LORA + MLA_ROPE + MIX_WIDTH

NSA_KV_GROUPS = 4
NSA_GROUP = N_HEADS // NSA_KV_GROUPS
NSA_KV = NSA_KV_GROUPS * HEAD_DIM
CMP_BLOCK = 32
CMP_STRIDE = 16
SEL_BLOCK = 64
N_SEL = 8
SEL_Q_BLOCK = 32
SEL_FORCE = 1e9
NSA_WINDOW = 512
PHI_HIDDEN = 256
NSA_IN = MIX_WIDTH + 6 * NSA_KV + 3 * N_HEADS + MIX_WIDTH

kernel_name = 'hybrid_swa_mla_nsa_deepnorm'


def _layernorm(x, g, b):
    xf = x.astype(jnp.float32)
    mu = xf.mean(-1, keepdims=True)
    var = jnp.square(xf - mu).mean(-1, keepdims=True)
    return ((xf - mu) * lax.rsqrt(var + LN_EPS) * g + b).astype(x.dtype)


def _rmsnorm(x, g):
    xf = x.astype(jnp.float32)
    return (xf * lax.rsqrt(jnp.mean(xf * xf, -1, keepdims=True) + RMS_EPS) * g).astype(x.dtype)


def _rope(x, pos):
    half = x.shape[-1] // 2
    inv = ROPE_THETA ** (-jnp.arange(half, dtype=jnp.float32) / half)
    ang = pos.astype(jnp.float32)[..., None] * inv
    cos = jnp.cos(ang)[:, :, None, :]
    sin = jnp.sin(ang)[:, :, None, :]
    x1 = x[..., :half].astype(jnp.float32)
    x2 = x[..., half:].astype(jnp.float32)
    return jnp.concatenate([x1 * cos - x2 * sin, x2 * cos + x1 * sin], -1).astype(x.dtype)


def _banded_gqa(q, k, v, window, sinks):
    B, S, G, R, D = q.shape
    nb = S // Q_BLOCK
    n_prev = -(-(window - 1) // Q_BLOCK)
    span = (n_prev + 1) * Q_BLOCK
    pad = n_prev * Q_BLOCK
    kp = jnp.pad(k, ((0, 0), (pad, 0), (0, 0), (0, 0)))
    vp = jnp.pad(v, ((0, 0), (pad, 0), (0, 0), (0, 0)))
    qb = q.reshape(B, nb, Q_BLOCK, G, R, D).transpose(1, 0, 2, 3, 4, 5)
    scale = D ** -0.5

    def blk(args):
        b, qblk = args
        start = b * Q_BLOCK
        kb = lax.dynamic_slice_in_dim(kp, start, span, axis=1)
        vb = lax.dynamic_slice_in_dim(vp, start, span, axis=1)
        t = start + jnp.arange(Q_BLOCK)
        j = start - pad + jnp.arange(span)
        mask = (j[None, :] <= t[:, None]) & (j[None, :] > t[:, None] - window) & (j[None, :] >= 0)
        s = jnp.einsum('bqgrd,bkgd->bgrqk', qblk, kb).astype(jnp.float32) * scale
        s = jnp.where(mask, s, NEG_INF)
        if sinks is None:
            pr = jax.nn.softmax(s, axis=-1)
        else:
            snk = jnp.broadcast_to(sinks.astype(jnp.float32).reshape(1, G, R, 1, 1), s.shape[:-1] + (1,))
            pr = jax.nn.softmax(jnp.concatenate([s, snk], -1), axis=-1)[..., :-1]
        return jnp.einsum('bgrqk,bkgd->bqgrd', pr.astype(vb.dtype), vb)

    ob = lax.map(blk, (jnp.arange(nb), qb))
    return ob.transpose(1, 0, 2, 3, 4, 5).reshape(B, S, G, R, D)


def _mla_attention(q_nope, q_rope, k_nope, k_rope, v):
    B, S, H, Dn = q_nope.shape
    Dr = q_rope.shape[-1]
    nb = S // Q_BLOCK
    scale = (Dn + Dr) ** -0.5
    qn = q_nope.reshape(B, nb, Q_BLOCK, H, Dn).transpose(1, 0, 2, 3, 4)
    qr = q_rope.reshape(B, nb, Q_BLOCK, H, Dr).transpose(1, 0, 2, 3, 4)
    kpos = jnp.arange(S)

    def blk(args):
        b, qnb, qrb = args
        t = b * Q_BLOCK + jnp.arange(Q_BLOCK)
        s = (jnp.einsum('bqhd,bkhd->bhqk', qnb, k_nope)
             + jnp.einsum('bqhd,bkd->bhqk', qrb, k_rope)).astype(jnp.float32) * scale
        pr = jax.nn.softmax(jnp.where(kpos[None, :] <= t[:, None], s, NEG_INF), axis=-1)
        return jnp.einsum('bhqk,bkhd->bqhd', pr.astype(v.dtype), v)

    ob = lax.map(blk, (jnp.arange(nb), qn, qr))
    return ob.transpose(1, 0, 2, 3, 4).reshape(B, S, H, v.shape[-1])


def _selected_attention(q, k, v, sel):
    B, S, G, R, D = q.shape
    K = sel.shape[-1]
    n_blk = S // SEL_BLOCK
    nq = S // SEL_Q_BLOCK
    scale = D ** -0.5
    kb = k.reshape(B, n_blk, SEL_BLOCK, G, D).transpose(0, 3, 1, 2, 4)
    vb = v.reshape(B, n_blk, SEL_BLOCK, G, D).transpose(0, 3, 1, 2, 4)
    qb = q.reshape(B, nq, SEL_Q_BLOCK, G, R, D).transpose(1, 0, 2, 3, 4, 5)
    ib = sel.reshape(B, G, nq, SEL_Q_BLOCK, K).transpose(2, 0, 1, 3, 4)
    gather = jax.vmap(jax.vmap(lambda blocks, ids: blocks[ids]))
    offs = jnp.arange(SEL_BLOCK)

    def blk(args):
        b, qblk, iblk = args
        t = b * SEL_Q_BLOCK + jnp.arange(SEL_Q_BLOCK)
        kg = gather(kb, iblk).reshape(B, G, SEL_Q_BLOCK, K * SEL_BLOCK, D)
        vg = gather(vb, iblk).reshape(B, G, SEL_Q_BLOCK, K * SEL_BLOCK, D)
        kpos = (iblk[..., None] * SEL_BLOCK + offs).reshape(B, G, SEL_Q_BLOCK, K * SEL_BLOCK)
        mask = (kpos <= t[None, None, :, None])[:, :, None]
        s = jnp.einsum('bqgrd,bgqkd->bgrqk', qblk, kg).astype(jnp.float32) * scale
        pr = jax.nn.softmax(jnp.where(mask, s, NEG_INF), axis=-1)
        return jnp.einsum('bgrqk,bgqkd->bqgrd', pr.astype(vg.dtype), vg)

    ob = lax.map(blk, (jnp.arange(nq), qb, ib))
    return ob.transpose(1, 0, 2, 3, 4, 5).reshape(B, S, G, R, D)


def _mixer_swa(h, positions, w_in, sinks):
    B, S, _ = h.shape
    proj = h @ w_in
    q, k, v, z = jnp.split(proj, [MIX_WIDTH, MIX_WIDTH + SWA_KV, MIX_WIDTH + 2 * SWA_KV], axis=-1)
    q = _rope(q.reshape(B, S, N_HEADS, HEAD_DIM), positions).reshape(B, S, SWA_KV_HEADS, SWA_GROUP, HEAD_DIM)
    k = _rope(k.reshape(B, S, SWA_KV_HEADS, HEAD_DIM), positions)
    v = v.reshape(B, S, SWA_KV_HEADS, HEAD_DIM)
    o = _banded_gqa(q, k, v, SWA_WINDOW, sinks).reshape(B, S, MIX_WIDTH)
    return o * jax.nn.silu(z)


def _mixer_mla(h, positions, w_in, q_norm, kv_norm, w_uq, w_ukv):
    B, S, _ = h.shape
    proj = h @ w_in
    cq, ckv, kr, z = jnp.split(proj, [MLA_Q_LORA, MLA_Q_LORA + MLA_KV_LORA,
                                      MLA_Q_LORA + MLA_KV_LORA + MLA_ROPE], axis=-1)
    q = (_rmsnorm(cq, q_norm) @ w_uq).reshape(B, S, N_HEADS, MLA_NOPE + MLA_ROPE)
    q_nope = q[..., :MLA_NOPE]
    q_rope = _rope(q[..., MLA_NOPE:], positions)
    kv = (_rmsnorm(ckv, kv_norm) @ w_ukv).reshape(B, S, N_HEADS, MLA_NOPE + MLA_V)
    k_nope = kv[..., :MLA_NOPE]
    v = kv[..., MLA_NOPE:]
    k_rope = _rope(kr[:, :, None, :], positions)[:, :, 0, :]
    o = _mla_attention(q_nope, q_rope, k_nope, k_rope, v).reshape(B, S, MIX_WIDTH)
    return o * jax.nn.silu(z)


def _mixer_nsa(h, positions, w_in, cmp_pos, phi_k1, phi_k2, phi_v1, phi_v2):
    B, S, _ = h.shape
    G, R, D = NSA_KV_GROUPS, NSA_GROUP, HEAD_DIM
    proj = h @ w_in
    cuts = np.cumsum([MIX_WIDTH] + [NSA_KV] * 6 + [3 * N_HEADS]).tolist()
    q, k_c, v_c, k_s, v_s, k_w, v_w, gl, z = jnp.split(proj, cuts, axis=-1)
    q = _rope(q.reshape(B, S, N_HEADS, D), positions).reshape(B, S, G, R, D)
    scale = D ** -0.5
    t_idx = jnp.arange(S)

    n_cmp = (S - CMP_BLOCK) // CMP_STRIDE + 1
    tok = jnp.arange(n_cmp)[:, None] * CMP_STRIDE + jnp.arange(CMP_BLOCK)[None, :]

    def compress(t, w1, w2):
        blocks = t.reshape(B, S, G, D)[:, tok] + cmp_pos[:, None, :]
        flat = blocks.transpose(0, 1, 3, 2, 4).reshape(B, n_cmp, G, CMP_BLOCK * D)
        return jax.nn.silu(flat @ w1) @ w2

    blk_end = tok[:, -1]
    kc = _rope(compress(k_c, phi_k1, phi_k2), positions[:, blk_end])
    vc = compress(v_c, phi_v1, phi_v2)
    cmask = blk_end[None, :] <= t_idx[:, None]
    sc = jnp.einsum('bsgrd,bngd->bgrsn', q, kc).astype(jnp.float32) * scale
    pc = jnp.where(cmask, jax.nn.softmax(jnp.where(cmask, sc, NEG_INF), axis=-1), 0.0)
    o_cmp = jnp.einsum('bgrsn,bngd->bsgrd', pc.astype(vc.dtype), vc)

    n_blk = S // SEL_BLOCK
    n_top = min(N_SEL, n_blk)
    j = jnp.arange(n_blk)
    cstart = tok[:, 0]
    overlap = ((cstart[:, None] < (j[None, :] + 1) * SEL_BLOCK)
               & (cstart[:, None] + CMP_BLOCK > j[None, :] * SEL_BLOCK)).astype(jnp.float32)
    imp = jnp.einsum('bgrsn,nj->bgsj', pc, overlap)
    cur = (t_idx // SEL_BLOCK)[:, None]
    forced = (j[None, :] == 0) | (j[None, :] == cur) | (j[None, :] == cur - 1)
    score = jnp.where(forced, SEL_FORCE, jnp.where(j[None, :] <= cur, imp, -SEL_FORCE))
    _, sel = lax.top_k(score, n_top)
    ks = _rope(k_s.reshape(B, S, G, D), positions)
    o_slc = _selected_attention(q, ks, v_s.reshape(B, S, G, D), sel)

    kw = _rope(k_w.reshape(B, S, G, D), positions)
    o_win = _banded_gqa(q, kw, v_w.reshape(B, S, G, D), NSA_WINDOW, None)

    g = jax.nn.sigmoid(gl.astype(jnp.float32)).astype(h.dtype).reshape(B, S, 3, G, R, 1)
    o = g[:, :, 0] * o_cmp + g[:, :, 1] * o_slc + g[:, :, 2] * o_win
    return o.reshape(B, S, MIX_WIDTH) * jax.nn.silu(z)


def setup_inputs(seed: int = 0) -> dict:
    key = jax.random.key(seed)
    ks = list(jax.random.split(key, 40))

    def nrm(shape, scale):
        return jax.random.normal(ks.pop(), shape, jnp.float32) * scale

    def gain(shape):
        return 1.0 + nrm(shape, 0.02)

    x = nrm((BATCH, SEQ, D_MODEL), 1.0)
    p = nrm((DEPTH, BATCH, SEQ, PLE_DIM), 1.0)
    positions = (jax.random.randint(ks.pop(), (BATCH, 1), 0, MAX_POS_OFFSET, dtype=jnp.int32)
                 + jnp.arange(SEQ, dtype=jnp.int32)[None, :]).astype(jnp.int32)
    return {
        'x': x,
        'p': p,
        'positions': positions,
        'w_out': nrm((DEPTH, MIX_WIDTH, D_MODEL), MIX_WIDTH ** -0.5 * DEEPNORM_BETA),
        'ln_g': gain((DEPTH, D_MODEL)),
        'ln_b': nrm((DEPTH, D_MODEL), 0.02),
        'pe_gate': nrm((DEPTH, D_MODEL, D_MODEL), D_MODEL ** -0.5),
        'pe_proj': nrm((DEPTH, PLE_DIM, D_MODEL), PLE_DIM ** -0.5),
        'l0_w_in': nrm((D_MODEL, SWA_IN), D_MODEL ** -0.5),
        'l0_sinks': nrm((N_HEADS,), 0.5),
        'l1_w_in': nrm((D_MODEL, MLA_IN), D_MODEL ** -0.5),
        'l1_q_norm': gain((MLA_Q_LORA,)),
        'l1_kv_norm': gain((MLA_KV_LORA,)),
        'l1_w_uq': nrm((MLA_Q_LORA, N_HEADS * (MLA_NOPE + MLA_ROPE)), MLA_Q_LORA ** -0.5),
        'l1_w_ukv': nrm((MLA_KV_LORA, N_HEADS * (MLA_NOPE + MLA_V)), MLA_KV_LORA ** -0.5),
        'l2_w_in': nrm((D_MODEL, NSA_IN), D_MODEL ** -0.5),
        'l2_cmp_pos': nrm((CMP_BLOCK, HEAD_DIM), 0.1),
        'l2_phi_k1': nrm((CMP_BLOCK * HEAD_DIM, PHI_HIDDEN), (CMP_BLOCK * HEAD_DIM) ** -0.5),
        'l2_phi_k2': nrm((PHI_HIDDEN, HEAD_DIM), PHI_HIDDEN ** -0.5),
        'l2_phi_v1': nrm((CMP_BLOCK * HEAD_DIM, PHI_HIDDEN), (CMP_BLOCK * HEAD_DIM) ** -0.5),
        'l2_phi_v2': nrm((PHI_HIDDEN, HEAD_DIM), PHI_HIDDEN ** -0.5),
        'l3_w_in': nrm((D_MODEL, SWA_IN), D_MODEL ** -0.5),
        'l3_sinks': nrm((N_HEADS,), 0.5),
    }


def reference(x, p, positions, w_out, ln_g, ln_b, pe_gate, pe_proj,
              l0_w_in, l0_sinks,
              l1_w_in, l1_q_norm, l1_kv_norm, l1_w_uq, l1_w_ukv,
              l2_w_in, l2_cmp_pos, l2_phi_k1, l2_phi_k2, l2_phi_v1, l2_phi_v2,
              l3_w_in, l3_sinks):
    mixers = (_mixer_swa, _mixer_mla, _mixer_nsa)
    layer_params = (
        (l0_w_in, l0_sinks),
        (l1_w_in, l1_q_norm, l1_kv_norm, l1_w_uq, l1_w_ukv),
        (l2_w_in, l2_cmp_pos, l2_phi_k1, l2_phi_k2, l2_phi_v1, l2_phi_v2),
        (l3_w_in, l3_sinks),
    )
    for i in range(DEPTH):
        y = mixers[i % N_MIXERS](x, positions, *layer_params[i]) @ w_out[i]
        x = _layernorm(DEEPNORM_ALPHA * x + y, ln_g[i], ln_b[i])
        x = x + jax.nn.sigmoid(x @ pe_gate[i]) * (p[i] @ pe_proj[i])
    return x
```

```python
import functools

import numpy as np
import jax
import jax.numpy as jnp
from jax import lax
from jax.experimental import pallas as pl
from jax.experimental.pallas import tpu as pltpu

F32 = jnp.float32
BF16 = jnp.bfloat16

LANES = 128
VMEM_LIMIT = 56 * 1024 * 1024

D_MODEL = 1024
DEPTH = 4
HEAD_DIM = 64
N_HEADS = 16
MIX_WIDTH = 1024
ROPE_THETA = 10000.0
NEG_INF = -1e30
PLE_DIM = 256
DEEPNORM_ALPHA = (2 * DEPTH) ** 0.25
LN_EPS = 1e-5
RMS_EPS = 1e-6

SWA_WINDOW = 128
KV_GROUPS = 4
KV_WIDTH = KV_GROUPS * HEAD_DIM

MLA_Q_LORA = 384
MLA_KV_LORA = 256
MLA_NOPE = 64
MLA_ROPE = 32
MLA_V = 64

CMP_BLOCK = 32
CMP_STRIDE = 16
SEL_BLOCK = 64
N_SEL = 8
SEL_FORCE = 1e9
NSA_WINDOW = 512
PHI_HIDDEN = 256

BAND_BLOCK = 128


def _cparams(sem):
    return pltpu.CompilerParams(dimension_semantics=sem, vmem_limit_bytes=VMEM_LIMIT)


def _pair_head(gp, r, half):
    return (2 * gp + half) * 4 + r


def _gqa_perms():
    qperm, kperm, operm = [], [], []
    for gp in range(2):
        for r in range(4):
            for part in range(2):
                for half in range(2):
                    h = _pair_head(gp, r, half)
                    qperm += [h * 64 + part * 32 + i for i in range(32)]
            for half in range(2):
                h = _pair_head(gp, r, half)
                operm += [h * 64 + d for d in range(64)]
        for part in range(2):
            for half in range(2):
                g = 2 * gp + half
                kperm += [g * 64 + part * 32 + i for i in range(32)]
    return np.array(qperm), np.array(kperm), np.array(operm)


_QPERM, _KPERM, _OPERM = _gqa_perms()
_POS_HEAD = np.array([_pair_head(gp, r, half) for gp in range(2) for r in range(4) for half in range(2)])


def _mla_perms():
    uq, kr = [], []
    for hg in range(4):
        for a in range(4):
            uq += [(4 * hg + a) * 96 + d for d in range(64)]
        for part in range(2):
            for a in range(4):
                uq += [(4 * hg + a) * 96 + 64 + part * 16 + i for i in range(16)]
    for part in range(2):
        for _ in range(4):
            kr += [part * 16 + i for i in range(16)]
    kn = [h * 128 + d for h in range(16) for d in range(64)]
    vv = [h * 128 + 64 + d for h in range(16) for d in range(64)]
    return np.array(uq), np.array(kr), np.array(kn), np.array(vv)


_UQPERM, _KRPERM, _KNPERM, _VPERM = _mla_perms()


def _rope_lane_tables(half):
    inv = ROPE_THETA ** (-jnp.arange(half, dtype=F32) / half)
    lane = np.arange(LANES)
    inv_l = inv[lane % half][None, :]
    sign = np.where(lane < 64, -1.0, 1.0).astype(np.float32)[None, :]
    return inv_l, jnp.asarray(sign)


def _rope_table_body(pos_ref, inv_ref, sign_ref, cos_ref, sin_ref):
    ang = pos_ref[...].astype(F32) * inv_ref[...]
    cos_ref[...] = jnp.cos(ang)
    sin_ref[...] = jnp.sin(ang) * sign_ref[...]


def _rope_tables(pos, half):
    n = pos.shape[0]
    tr = min(n, 1024)
    inv_l, sign = _rope_lane_tables(half)
    row = pl.BlockSpec((tr, LANES), lambda i: (i, 0))
    cst = pl.BlockSpec((1, LANES), lambda i: (0, 0))
    return pl.pallas_call(
        _rope_table_body,
        grid=(n // tr,),
        in_specs=[pl.BlockSpec((tr, 1), lambda i: (i, 0)), cst, cst],
        out_specs=[row, row],
        out_shape=[jax.ShapeDtypeStruct((n, LANES), F32)] * 2,
        compiler_params=_cparams(("parallel",)),
        name="rope_tables",
    )(pos.reshape(n, 1), inv_l, sign)


def _rope_cols(v, cos, sin):
    return v * cos + pltpu.roll(v, 64, 1) * sin


def _proj_gqa_body(n_kv, with_cmp, x_ref, w_ref, cos_ref, sin_ref, *outs):
    xb = x_ref[0].astype(BF16)
    cos = cos_ref[0]
    sin = sin_ref[0]

    def mm(c0, width):
        return jnp.dot(xb, w_ref[:, c0:c0 + width], preferred_element_type=F32)

    q_ref, z_ref = outs[0], outs[1]
    for c in range(0, MIX_WIDTH, 512):
        acc = mm(c, 512)
        for j in range(0, 512, LANES):
            q_ref[0, :, c + j:c + j + LANES] = _rope_cols(acc[:, j:j + LANES], cos, sin).astype(BF16)
    for c in range(0, MIX_WIDTH, 512):
        z_ref[0, :, c:c + 512] = mm(MIX_WIDTH + c, 512).astype(BF16)
    col = 2 * MIX_WIDTH
    o = 2
    for _ in range(n_kv):
        acc = mm(col, 2 * KV_WIDTH)
        for j in range(0, KV_WIDTH, LANES):
            outs[o][0, :, j:j + LANES] = _rope_cols(acc[:, j:j + LANES], cos, sin).astype(BF16)
        outs[o + 1][0] = acc[:, KV_WIDTH:].astype(BF16)
        col += 2 * KV_WIDTH
        o += 2
    if with_cmp:
        acc = mm(col, 2 * KV_WIDTH)
        for t in range(2):
            for gp in range(2):
                c0 = t * KV_WIDTH + gp * LANES
                outs[o + t][0, gp] = acc[:, c0:c0 + LANES]
        col += 2 * KV_WIDTH
        outs[o + 2][0] = jax.nn.sigmoid(mm(col, 2 * LANES))


def _proj_gqa(x, w, cos, sin, n_kv, with_cmp, tm=256):
    B, S, _ = x.shape
    n = w.shape[1]
    tok = lambda width: pl.BlockSpec((1, tm, width), lambda b, i: (b, i, 0))
    out_specs = [tok(MIX_WIDTH), tok(MIX_WIDTH)]
    out_shape = [jax.ShapeDtypeStruct((B, S, MIX_WIDTH), BF16)] * 2
    for _ in range(n_kv):
        out_specs += [tok(KV_WIDTH), tok(KV_WIDTH)]
        out_shape += [jax.ShapeDtypeStruct((B, S, KV_WIDTH), BF16)] * 2
    if with_cmp:
        raw = pl.BlockSpec((1, 2, tm, LANES), lambda b, i: (b, 0, i, 0))
        out_specs += [raw, raw, tok(2 * LANES)]
        out_shape += [jax.ShapeDtypeStruct((B, 2, S, LANES), F32)] * 2
        out_shape += [jax.ShapeDtypeStruct((B, S, 2 * LANES), F32)]
    return pl.pallas_call(
        functools.partial(_proj_gqa_body, n_kv, with_cmp),
        grid=(B, S // tm),
        in_specs=[tok(D_MODEL), pl.BlockSpec((D_MODEL, n), lambda b, i: (0, 0)), tok(LANES), tok(LANES)],
        out_specs=out_specs,
        out_shape=out_shape,
        compiler_params=_cparams(("parallel", "parallel")),
        name="proj_gqa",
    )(x, w, cos, sin)


def _rms(v, g):
    return v * lax.rsqrt(jnp.mean(v * v, -1, keepdims=True) + RMS_EPS) * g


def _proj_mla_body(x_ref, w1_ref, qn_ref, kvn_ref, wuq_ref, wkv_ref, cos_ref, sin_ref,
                   q_ref, k_ref, v_ref, kr_ref, z_ref):
    xb = x_ref[0].astype(BF16)
    cos = cos_ref[0]
    sin = sin_ref[0]
    lat = MLA_Q_LORA + MLA_KV_LORA
    c = jnp.dot(xb, w1_ref[:, :lat + LANES], preferred_element_type=F32)
    cq = _rms(c[:, :MLA_Q_LORA], qn_ref[...]).astype(BF16)
    ckv = _rms(c[:, MLA_Q_LORA:lat], kvn_ref[...]).astype(BF16)
    kr_ref[0] = _rope_cols(c[:, lat:lat + LANES], cos, sin).astype(BF16)
    scale = (MLA_NOPE + MLA_ROPE) ** -0.5
    for hg in range(4):
        c0 = hg * 384
        q = jnp.dot(cq, wuq_ref[:, c0:c0 + 384], preferred_element_type=F32) * scale
        q_ref[0, :, c0:c0 + 256] = q[:, :256].astype(BF16)
        q_ref[0, :, c0 + 256:c0 + 384] = _rope_cols(q[:, 256:], cos, sin).astype(BF16)
    for c0 in range(0, MIX_WIDTH, 512):
        k_ref[0, :, c0:c0 + 512] = jnp.dot(ckv, wkv_ref[:, c0:c0 + 512], preferred_element_type=F32).astype(BF16)
        v_ref[0, :, c0:c0 + 512] = jnp.dot(ckv, wkv_ref[:, MIX_WIDTH + c0:MIX_WIDTH + c0 + 512],
                                            preferred_element_type=F32).astype(BF16)
        z_ref[0, :, c0:c0 + 512] = jnp.dot(xb, w1_ref[:, lat + LANES + c0:lat + LANES + c0 + 512],
                                            preferred_element_type=F32).astype(BF16)


def _proj_mla(x, w1, qn, kvn, wuq, wkv, cos, sin, tm=256):
    B, S, _ = x.shape
    tok = lambda width: pl.BlockSpec((1, tm, width), lambda b, i: (b, i, 0))
    full = lambda a: pl.BlockSpec(a.shape, lambda b, i: (0,) * a.ndim)
    widths = [16 * 96, MIX_WIDTH, MIX_WIDTH, LANES, MIX_WIDTH]
    return pl.pallas_call(
        _proj_mla_body,
        grid=(B, S // tm),
        in_specs=[tok(D_MODEL), full(w1), full(qn), full(kvn), full(wuq), full(wkv), tok(LANES), tok(LANES)],
        out_specs=[tok(w) for w in widths],
        out_shape=[jax.ShapeDtypeStruct((B, S, w), BF16) for w in widths],
        compiler_params=_cparams(("parallel", "parallel")),
        name="proj_mla",
    )(x, w1, qn, kvn, wuq, wkv, cos, sin)


def _lane_iota():
    return lax.broadcasted_iota(jnp.int32, (1, LANES), 1)


def _online_step(lhs, kt, v_lo, v_hi, mask, state, n_pair, tq):
    m, l, acc = state
    tk = kt.shape[0]
    s = lax.dot_general(lhs, kt, (((1,), (1,)), ((), ())), preferred_element_type=F32)
    s = jnp.where(mask, s.reshape(2 * n_pair, tq, tk), NEG_INF)
    m_new = jnp.maximum(m, jnp.max(s, -1, keepdims=True))
    alpha = jnp.exp(m - m_new)
    p = jnp.exp(s - m_new)
    l = alpha * l + jnp.sum(p, -1, keepdims=True)
    pb = p.astype(BF16).reshape(2 * n_pair * tq, tk)
    pv = (jnp.dot(pb[:n_pair * tq], v_lo, preferred_element_type=F32)
          + jnp.dot(pb[n_pair * tq:], v_hi, preferred_element_type=F32))
    lo = _lane_iota()[None] < 64
    acc = acc * jnp.where(lo, alpha[:n_pair], alpha[n_pair:]) + pv.reshape(n_pair, tq, LANES)
    return m_new, l, acc


def _split_halves(vt):
    lo = _lane_iota() < 64
    return jnp.where(lo, vt, 0), jnp.where(lo, 0, vt)


def _finish(state, n_pair, gates):
    _, l, acc = state
    inv = 1.0 / l
    if gates is not None:
        g, cols_a, cols_b = gates
        ga = jnp.stack([g[:, c:c + 1] for c in cols_a])
        gb = jnp.stack([g[:, c:c + 1] for c in cols_b])
        inv = inv * jnp.concatenate([ga, gb], 0)
    lo = _lane_iota()[None] < 64
    return acc * jnp.where(lo, inv[:n_pair], inv[n_pair:])


def _flash_gqa_body(cfg, *refs):
    tq, S, mode, window, use_sel, use_sink, gate_base, tk = cfg
    refs = list(refs)
    q_ref, k_ref, v_ref = refs[:3]
    nxt = 3
    sb_ref = gate_ref = sink_ref = None
    if use_sel:
        sb_ref = refs[nxt]; nxt += 1
    if gate_base is not None:
        gate_ref = refs[nxt]; nxt += 1
    if use_sink:
        sink_ref = refs[nxt]; nxt += 1
    o_ref = refs[nxt]

    gp = pl.program_id(1)
    q0 = pl.program_id(2) * tq
    lane = _lane_iota()
    m_a = ((lane >> 5) & 1) == 0
    a_rows, b_rows = [], []
    for r in range(4):
        qc = q_ref[0, :, r * LANES:(r + 1) * LANES]
        a = jnp.where(m_a, qc, 0)
        b = jnp.where(m_a, 0, qc)
        if use_sel:
            a = jnp.concatenate([a, sb_ref[0, :, :LANES]], 1)
            b = jnp.concatenate([b, sb_ref[0, :, LANES:]], 1)
        a_rows.append(a)
        b_rows.append(b)
    lhs = jnp.concatenate(a_rows + b_rows, 0)

    if use_sink:
        m0 = jnp.concatenate(
            [jnp.full((1, tq, 1), sink_ref[(2 * gp + half) * 4 + r], F32) for half in range(2) for r in range(4)], 0)
        l0 = jnp.ones((8, tq, 1), F32)
    else:
        m0 = jnp.full((8, tq, 1), NEG_INF, F32)
        l0 = jnp.zeros((8, tq, 1), F32)
    state = (m0, l0, jnp.zeros((4, tq, LANES), F32))
    tpos = q0 + lax.broadcasted_iota(jnp.int32, (1, tq, 1), 1)

    def step(start, state):
        start = pl.multiple_of(start, BAND_BLOCK)
        kt = k_ref[0, pl.ds(start, tk), :]
        vt = v_ref[0, pl.ds(start, tk), :]
        kpos = start + lax.broadcasted_iota(jnp.int32, (1, 1, tk), 2)
        mask = kpos <= tpos
        if mode == "band":
            mask = mask & (kpos > tpos - window)
        if use_sel:
            blk = (start + lax.broadcasted_iota(jnp.int32, (tk, LANES), 0)) >> 6
            onehot = (blk == lax.broadcasted_iota(jnp.int32, (tk, LANES), 1)).astype(BF16)
            kt = jnp.concatenate([kt, onehot], 1)
        v_lo, v_hi = _split_halves(vt)
        return _online_step(lhs, kt, v_lo, v_hi, mask, state, 4, tq)

    if mode == "band":
        state = step(jnp.clip(q0 - (tk - tq), 0, S - tk), state)
    else:
        n_tiles = (q0 + tq + tk - 1) // tk
        state = lax.fori_loop(0, n_tiles, lambda j, st: step(j * tk, st), state)

    gates = None
    if gate_base is not None:
        cols_a = [gate_base + 2 * r for r in range(4)]
        cols_b = [gate_base + 2 * r + 1 for r in range(4)]
        gates = (gate_ref[0], cols_a, cols_b)
    out = _finish(state, 4, gates)
    for r in range(4):
        o_ref[0, :, r * LANES:(r + 1) * LANES] = out[r].astype(o_ref.dtype)


def _flash_gqa(q, k, v, *, mode, window=None, sel=None, gate=None, gate_base=None, sinks=None, tq=128, tk=512):
    B, S, _ = q.shape
    if mode == "band":
        pad = -(-(window - 1) // BAND_BLOCK) * BAND_BLOCK
        tk = pad + tq
    cfg = (tq, S, mode, window, sel is not None, sinks is not None, gate_base if gate is not None else None, tk)
    qspec = pl.BlockSpec((1, tq, 512), lambda b, g, i: (b, i, g))
    kvspec = pl.BlockSpec((1, S, LANES), lambda b, g, i: (b, 0, g))
    in_specs = [qspec, kvspec, kvspec]
    args = [q, k, v]
    if sel is not None:
        in_specs.append(pl.BlockSpec((1, tq, 2 * LANES), lambda b, g, i: (b, i, g)))
        args.append(sel)
    if gate is not None:
        in_specs.append(pl.BlockSpec((1, tq, LANES), lambda b, g, i: (b, i, g)))
        args.append(gate)
    if sinks is not None:
        in_specs.append(pl.BlockSpec(memory_space=pltpu.SMEM))
        args.append(sinks)
    return pl.pallas_call(
        functools.partial(_flash_gqa_body, cfg),
        grid=(B, 2, S // tq),
        in_specs=in_specs,
        out_specs=qspec,
        out_shape=jax.ShapeDtypeStruct((B, S, MIX_WIDTH), BF16),
        compiler_params=_cparams(("parallel", "parallel", "arbitrary")),
        name=f"attn_{mode}{'_sel' if sel is not None else ''}{window or ''}",
    )(*args)


def _flash_mla_body(cfg, q_ref, k_ref, v_ref, kr_ref, o_ref):
    tq, S, tk = cfg
    q0 = pl.program_id(2) * tq
    lane = _lane_iota()
    lo = lane < 64
    slot = (lane >> 4) & 3
    qr = q_ref[0, :, 256:384]
    lhs = []
    for pair in range(2):
        qn = q_ref[0, :, pair * LANES:(pair + 1) * LANES]
        a = jnp.concatenate([jnp.where(lo, qn, 0), jnp.where(slot == 2 * pair, qr, 0)], 1)
        b = jnp.concatenate([jnp.where(lo, 0, qn), jnp.where(slot == 2 * pair + 1, qr, 0)], 1)
        lhs.append(jnp.concatenate([a, b], 0))
    tpos = q0 + lax.broadcasted_iota(jnp.int32, (1, tq, 1), 1)

    def init():
        return (jnp.full((2, tq, 1), NEG_INF, F32), jnp.zeros((2, tq, 1), F32), jnp.zeros((1, tq, LANES), F32))

    def step(j, states):
        start = pl.multiple_of(j * tk, BAND_BLOCK)
        kr = kr_ref[0, pl.ds(start, tk), :]
        kpos = start + lax.broadcasted_iota(jnp.int32, (1, 1, tk), 2)
        mask = kpos <= tpos
        new = []
        for pair in range(2):
            kt = jnp.concatenate([k_ref[0, pl.ds(start, tk), pair * LANES:(pair + 1) * LANES], kr], 1)
            v_lo, v_hi = _split_halves(v_ref[0, pl.ds(start, tk), pair * LANES:(pair + 1) * LANES])
            new.append(_online_step(lhs[pair], kt, v_lo, v_hi, mask, states[pair], 1, tq))
        return tuple(new)

    n_tiles = (q0 + tq + tk - 1) // tk
    states = lax.fori_loop(0, n_tiles, step, (init(), init()))
    for pair in range(2):
        o_ref[0, :, pair * LANES:(pair + 1) * LANES] = _finish(states[pair], 1, None)[0].astype(o_ref.dtype)


def _flash_mla(q, k, v, kr, tq=256, tk=512):
    B, S, _ = q.shape
    return pl.pallas_call(
        functools.partial(_flash_mla_body, (tq, S, tk)),
        grid=(B, 4, S // tq),
        in_specs=[pl.BlockSpec((1, tq, 384), lambda b, g, i: (b, i, g)),
                  pl.BlockSpec((1, S, 256), lambda b, g, i: (b, 0, g)),
                  pl.BlockSpec((1, S, 256), lambda b, g, i: (b, 0, g)),
                  pl.BlockSpec((1, S, LANES), lambda b, g, i: (b, 0, 0))],
        out_specs=pl.BlockSpec((1, tq, 256), lambda b, g, i: (b, i, g)),
        out_shape=jax.ShapeDtypeStruct((B, S, MIX_WIDTH), BF16),
        compiler_params=_cparams(("parallel", "parallel", "arbitrary")),
        name="attn_mla",
    )(q, k, v, kr)


def _compress_body(kraw_ref, vraw_ref, posa_ref, posb_ref, wk1a_ref, wk1b_ref, wk2_ref,
                   wv1a_ref, wv1b_ref, wv2_ref, cos_ref, sin_ref, kc_ref, vc_ref):
    n = kraw_ref.shape[2]

    def phi(raw, w1a, w1b, w2):
        a = jnp.dot((raw + posa_ref[...]).astype(BF16), w1a[...], preferred_element_type=F32)
        b = jnp.dot((raw + posb_ref[...]).astype(BF16), w1b[...], preferred_element_type=F32)
        h = a + pltpu.roll(b, n - 1, 0)
        return jnp.dot(jax.nn.silu(h).astype(BF16), w2[...], preferred_element_type=F32)

    kc = phi(kraw_ref[0, 0], wk1a_ref, wk1b_ref, wk2_ref)
    kc_ref[0, 0] = _rope_cols(kc, cos_ref[0], sin_ref[0]).astype(BF16)
    vc_ref[0, 0] = phi(vraw_ref[0, 0], wv1a_ref, wv1b_ref, wv2_ref).astype(BF16)


def _compress(kraw, vraw, posa, posb, wk, wv, cos_c, sin_c):
    B = kraw.shape[0]
    n = kraw.shape[2]
    raw = pl.BlockSpec((1, 1, n, kraw.shape[3]), lambda b, g: (b, g, 0, 0))
    full = lambda a: pl.BlockSpec(a.shape, lambda b, g: (0,) * a.ndim)
    tab = pl.BlockSpec((1, n, LANES), lambda b, g: (b, 0, 0))
    out = pl.BlockSpec((1, 1, n, LANES), lambda b, g: (b, g, 0, 0))
    return pl.pallas_call(
        _compress_body,
        grid=(B, 2),
        in_specs=[raw, raw, full(posa), full(posb)] + [full(w) for w in wk] + [full(w) for w in wv] + [tab, tab],
        out_specs=[out, out],
        out_shape=[jax.ShapeDtypeStruct((B, 2, n, LANES), BF16)] * 2,
        compiler_params=_cparams(("parallel", "parallel")),
        name="nsa_compress",
    )(kraw, vraw, posa, posb, *wk, *wv, cos_c, sin_c)


def _split3(p):
    hi = p.astype(BF16)
    r1 = p - hi.astype(F32)
    mid = r1.astype(BF16)
    lo = (r1 - mid.astype(F32)).astype(BF16)
    return hi, mid, lo


def _cmp_select_body(cfg, q_ref, kc_ref, vc_ref, ov_ref, gate_ref, o_ref, sb_ref):
    tq, n_cmp_pad, n_blk = cfg
    gp = pl.program_id(1)
    q0 = pl.program_id(2) * tq
    lane = _lane_iota()
    m_a = ((lane >> 5) & 1) == 0
    a_rows, b_rows = [], []
    for r in range(4):
        qc = q_ref[0, :, r * LANES:(r + 1) * LANES]
        a_rows.append(jnp.where(m_a, qc, 0))
        b_rows.append(jnp.where(m_a, 0, qc))
    lhs = jnp.concatenate(a_rows + b_rows, 0)
    kc = kc_ref[0, 0]
    s = lax.dot_general(lhs, kc, (((1,), (1,)), ((), ())), preferred_element_type=F32)
    s = s.reshape(8, tq, n_cmp_pad)
    tpos = q0 + lax.broadcasted_iota(jnp.int32, (1, tq, 1), 1)
    blk_end = lax.broadcasted_iota(jnp.int32, (1, 1, n_cmp_pad), 2) * CMP_STRIDE + (CMP_BLOCK - 1)
    cmask = blk_end <= tpos
    s = jnp.where(cmask, s, NEG_INF)
    m = jnp.max(s, -1, keepdims=True)
    p = jnp.where(cmask, jnp.exp(s - m), 0.0)
    l = jnp.sum(p, -1, keepdims=True)
    pc = p * jnp.where(l > 0.0, 1.0 / l, 0.0)

    v_lo, v_hi = _split_halves(vc_ref[0, 0])
    pb = pc.astype(BF16).reshape(8 * tq, n_cmp_pad)
    pv = (jnp.dot(pb[:4 * tq], v_lo, preferred_element_type=F32)
          + jnp.dot(pb[4 * tq:], v_hi, preferred_element_type=F32)).reshape(4, tq, LANES)
    g = gate_ref[0]
    ga = jnp.stack([g[:, 2 * r:2 * r + 1] for r in range(4)])
    gb = jnp.stack([g[:, 2 * r + 1:2 * r + 2] for r in range(4)])
    out = pv * jnp.where(lane[None] < 64, ga, gb)
    for r in range(4):
        o_ref[0, :, r * LANES:(r + 1) * LANES] = out[r].astype(o_ref.dtype)

    t2 = q0 + lax.broadcasted_iota(jnp.int32, (tq, 1), 0)
    cur = t2 >> 6
    forced = (lane == 0) | (lane == cur) | (lane == cur - 1)
    ov = ov_ref[...]
    for half in range(2):
        psum = pc[4 * half] + pc[4 * half + 1] + pc[4 * half + 2] + pc[4 * half + 3]
        imp = sum(jnp.dot(part, ov, preferred_element_type=F32) for part in _split3(psum))
        score = jnp.where(forced, SEL_FORCE, jnp.where(lane <= cur, imp, -SEL_FORCE))
        score = jnp.where(lane < n_blk, score, -2.0 * SEL_FORCE)
        rank = jnp.zeros((tq, LANES), jnp.int32)
        for j in range(n_blk):
            col = score[:, j:j + 1]
            beats = (col > score) | ((col == score) & (lane > j))
            rank = rank + beats.astype(jnp.int32)
        keep = (rank < min(N_SEL, n_blk)) | (lane >= n_blk)
        sb_ref[0, :, half * LANES:(half + 1) * LANES] = jnp.where(keep, 0.0, NEG_INF).astype(BF16)


def _cmp_select(q, kc, vc, ov, gate, tq=128):
    B, S, _ = q.shape
    n_cmp_pad = kc.shape[2]
    n_blk = S // SEL_BLOCK
    qspec = pl.BlockSpec((1, tq, 512), lambda b, g, i: (b, i, g))
    cspec = pl.BlockSpec((1, 1, n_cmp_pad, LANES), lambda b, g, i: (b, g, 0, 0))
    return pl.pallas_call(
        functools.partial(_cmp_select_body, (tq, n_cmp_pad, n_blk)),
        grid=(B, 2, S // tq),
        in_specs=[qspec, cspec, cspec, pl.BlockSpec(ov.shape, lambda b, g, i: (0, 0)),
                  pl.BlockSpec((1, tq, LANES), lambda b, g, i: (b, i, g))],
        out_specs=[qspec, pl.BlockSpec((1, tq, 2 * LANES), lambda b, g, i: (b, i, g))],
        out_shape=[jax.ShapeDtypeStruct((B, S, MIX_WIDTH), BF16), jax.ShapeDtypeStruct((B, S, 4 * LANES), BF16)],
        compiler_params=_cparams(("parallel", "parallel", "parallel")),
        name="nsa_cmp_select",
    )(q, kc, vc, ov, gate)


def _out_body(n_o, *refs):
    o_refs = refs[:n_o]
    z_ref, x_ref, p_ref, wo_ref, g_ref, b_ref, wg_ref, wp_ref, out_ref = refs[n_o:]
    o = o_refs[0][0].astype(F32)
    for r in o_refs[1:]:
        o = o + r[0].astype(F32)
    gated = (o * jax.nn.silu(z_ref[0].astype(F32))).astype(BF16)
    y = jnp.dot(gated, wo_ref[...], preferred_element_type=F32)
    u = DEEPNORM_ALPHA * x_ref[0] + y
    mu = jnp.mean(u, -1, keepdims=True)
    var = jnp.mean(jnp.square(u - mu), -1, keepdims=True)
    xn = (u - mu) * lax.rsqrt(var + LN_EPS) * g_ref[...] + b_ref[...]
    gate = jax.nn.sigmoid(jnp.dot(xn.astype(BF16), wg_ref[...], preferred_element_type=F32))
    emb = jnp.dot(p_ref[0].astype(BF16), wp_ref[...], preferred_element_type=F32)
    out_ref[0] = xn + gate * emb


def _out_block(o_list, z, x, p, wo, g, b, wg, wp, tm=256):
    B, S, _ = x.shape
    tok = lambda width: pl.BlockSpec((1, tm, width), lambda bb, i: (bb, i, 0))
    full = lambda a: pl.BlockSpec(a.shape, lambda bb, i: (0,) * a.ndim)
    n_o = len(o_list)
    return pl.pallas_call(
        functools.partial(_out_body, n_o),
        grid=(B, S // tm),
        in_specs=[tok(MIX_WIDTH)] * n_o + [tok(MIX_WIDTH), tok(D_MODEL), tok(PLE_DIM),
                                           full(wo), full(g), full(b), full(wg), full(wp)],
        out_specs=tok(D_MODEL),
        out_shape=jax.ShapeDtypeStruct((B, S, D_MODEL), F32),
        compiler_params=_cparams(("parallel", "parallel")),
        name="out_block",
    )(*o_list, z, x, p, wo, g, b, wg, wp)


def _swa_weights(w_in):
    q, k, v, z = jnp.split(w_in, [MIX_WIDTH, MIX_WIDTH + KV_WIDTH, MIX_WIDTH + 2 * KV_WIDTH], axis=1)
    q = q[:, _QPERM] * (HEAD_DIM ** -0.5)
    return jnp.concatenate([q, z[:, _OPERM], k[:, _KPERM], v], 1).astype(BF16)


def _nsa_weights(w_in):
    cuts = np.cumsum([MIX_WIDTH] + [KV_WIDTH] * 6 + [3 * N_HEADS]).tolist()
    q, k_c, v_c, k_s, v_s, k_w, v_w, gl, z = jnp.split(w_in, cuts, axis=1)
    q = q[:, _QPERM] * (HEAD_DIM ** -0.5)
    gl = jnp.concatenate(
        [jnp.pad(gl[:, np.array([br * N_HEADS + _POS_HEAD[gp * 8 + c] for br in range(3) for c in range(8)])],
                 ((0, 0), (0, LANES - 24))) for gp in range(2)], 1)
    return jnp.concatenate([q, z[:, _OPERM], k_s[:, _KPERM], v_s, k_w[:, _KPERM], v_w, k_c, v_c, gl], 1).astype(BF16)


def _phi_weights(w1, w2, rope_layout):
    h = w1.shape[1]
    w1 = w1.reshape(2, CMP_STRIDE, HEAD_DIM, h)
    eye = jnp.eye(2, dtype=w1.dtype)
    w1a, w1b = (jnp.einsum("ldj,gh->lgdhj", w1[t], eye).reshape(CMP_STRIDE * 2 * HEAD_DIM, 2 * h) for t in range(2))
    if rope_layout:
        w2 = jnp.einsum("jpi,gh->gjphi", w2.reshape(h, 2, 32), eye).reshape(2 * h, LANES)
    else:
        w2 = jnp.einsum("jd,gh->gjhd", w2, eye).reshape(2 * h, LANES)
    return w1a.astype(BF16), w1b.astype(BF16), w2.astype(BF16)


def _overlap_matrix(n_cmp, n_cmp_pad, n_blk):
    cstart = np.arange(n_cmp_pad) * CMP_STRIDE
    j = np.arange(LANES)
    ov = (cstart[:, None] < (j[None, :] + 1) * SEL_BLOCK) & (cstart[:, None] + CMP_BLOCK > j[None, :] * SEL_BLOCK)
    ov &= (np.arange(n_cmp_pad)[:, None] < n_cmp) & (j[None, :] < n_blk)
    return jnp.asarray(ov.astype(np.float32), dtype=BF16)


def _layer_swa(x, tabs64, w_in, sinks):
    q, z, k, v = _proj_gqa(x, _swa_weights(w_in), *tabs64, n_kv=1, with_cmp=False)
    o = _flash_gqa(q, k, v, mode="band", window=SWA_WINDOW, sinks=sinks.astype(F32))
    return [o], z, _OPERM


def _layer_mla(x, tabs32, w_in, q_norm, kv_norm, w_uq, w_ukv):
    lat = MLA_Q_LORA + MLA_KV_LORA
    w1 = jnp.concatenate([w_in[:, :lat], w_in[:, lat:lat + MLA_ROPE][:, _KRPERM], w_in[:, lat + MLA_ROPE:]], 1)
    wkv = jnp.concatenate([w_ukv[:, _KNPERM], w_ukv[:, _VPERM]], 1)
    q, k, v, kr, z = _proj_mla(x, w1.astype(BF16), q_norm[None, :], kv_norm[None, :],
                               w_uq[:, _UQPERM].astype(BF16), wkv.astype(BF16), *tabs32)
    return [_flash_mla(q, k, v, kr)], z, None


def _layer_nsa(x, tabs64, positions, w_in, cmp_pos, phi_k1, phi_k2, phi_v1, phi_v2):
    B, S, _ = x.shape
    q, z, ks, vs, kw, vw, kraw, vraw, gate = _proj_gqa(x, _nsa_weights(w_in), *tabs64, n_kv=2, with_cmp=True)
    n_chunk = S // CMP_STRIDE
    n_cmp = (S - CMP_BLOCK) // CMP_STRIDE + 1
    kraw = kraw.reshape(B, 2, n_chunk, CMP_STRIDE * LANES)
    vraw = vraw.reshape(B, 2, n_chunk, CMP_STRIDE * LANES)
    pos2 = jnp.tile(cmp_pos.reshape(2, CMP_STRIDE, 1, HEAD_DIM), (1, 1, 2, 1)).reshape(2, 1, CMP_STRIDE * LANES)
    end = jnp.minimum(jnp.arange(n_chunk) * CMP_STRIDE + CMP_BLOCK - 1, S - 1)
    cos_c, sin_c = _rope_tables(positions[:, end].reshape(-1), HEAD_DIM // 2)
    kc, vc = _compress(kraw, vraw, pos2[0], pos2[1], _phi_weights(phi_k1, phi_k2, True),
                       _phi_weights(phi_v1, phi_v2, False),
                       cos_c.reshape(B, n_chunk, LANES), sin_c.reshape(B, n_chunk, LANES))
    ov = _overlap_matrix(n_cmp, n_chunk, S // SEL_BLOCK)
    o_cmp, sel = _cmp_select(q, kc, vc, ov, gate)
    o_slc = _flash_gqa(q, ks, vs, mode="causal", sel=sel, gate=gate, gate_base=8)
    o_win = _flash_gqa(q, kw, vw, mode="band", window=NSA_WINDOW, gate=gate, gate_base=16)
    return [o_cmp, o_slc, o_win], z, _OPERM


def kernel(x, p, positions, w_out, ln_g, ln_b, pe_gate, pe_proj, l0_w_in, l0_sinks, l1_w_in, l1_q_norm, l1_kv_norm, l1_w_uq, l1_w_ukv, l2_w_in, l2_cmp_pos, l2_phi_k1, l2_phi_k2, l2_phi_v1, l2_phi_v2, l3_w_in, l3_sinks):
    B, S, _ = x.shape
    pos_flat = positions.reshape(-1)
    tabs64 = [t.reshape(B, S, LANES) for t in _rope_tables(pos_flat, HEAD_DIM // 2)]
    tabs32 = [t.reshape(B, S, LANES) for t in _rope_tables(pos_flat, MLA_ROPE // 2)]
    layers = (
        lambda h: _layer_swa(h, tabs64, l0_w_in, l0_sinks),
        lambda h: _layer_mla(h, tabs32, l1_w_in, l1_q_norm, l1_kv_norm, l1_w_uq, l1_w_ukv),
        lambda h: _layer_nsa(h, tabs64, positions, l2_w_in, l2_cmp_pos, l2_phi_k1, l2_phi_k2, l2_phi_v1, l2_phi_v2),
        lambda h: _layer_swa(h, tabs64, l3_w_in, l3_sinks),
    )
    for i, layer in enumerate(layers):
        o_list, z, operm = layer(x)
        wo = w_out[i] if operm is None else w_out[i][operm, :]
        x = _out_block(o_list, z, x, p[i], wo.astype(BF16), ln_g[i][None, :], ln_b[i][None, :],
                       pe_gate[i].astype(BF16), pe_proj[i].astype(BF16))
    return x
```

```python
import functools

import numpy as np
import jax
import jax.numpy as jnp
from jax import lax
from jax.experimental import pallas as pl
from jax.experimental.pallas import tpu as pltpu

F32 = jnp.float32
BF16 = jnp.bfloat16

LANES = 128
VMEM_LIMIT = 56 * 1024 * 1024

D_MODEL = 1024
DEPTH = 4
HEAD_DIM = 64
N_HEADS = 16
MIX_WIDTH = 1024
ROPE_THETA = 10000.0
NEG_INF = -1e30
PLE_DIM = 256
DEEPNORM_ALPHA = (2 * DEPTH) ** 0.25
LN_EPS = 1e-5
RMS_EPS = 1e-6

SWA_WINDOW = 128
KV_GROUPS = 4
KV_WIDTH = KV_GROUPS * HEAD_DIM

MLA_Q_LORA = 384
MLA_KV_LORA = 256
MLA_NOPE = 64
MLA_ROPE = 32
MLA_V = 64

CMP_BLOCK = 32
CMP_STRIDE = 16
SEL_BLOCK = 64
N_SEL = 8
SEL_FORCE = 1e9
NSA_WINDOW = 512
PHI_HIDDEN = 256

BAND_BLOCK = 128


def _cparams(sem):
    return pltpu.CompilerParams(dimension_semantics=sem, vmem_limit_bytes=VMEM_LIMIT)


def _pair_head(gp, r, half):
    return (2 * gp + half) * 4 + r


def _gqa_perms():
    qperm, kperm, operm = [], [], []
    for gp in range(2):
        for r in range(4):
            for part in range(2):
                for half in range(2):
                    h = _pair_head(gp, r, half)
                    qperm += [h * 64 + part * 32 + i for i in range(32)]
            for half in range(2):
                h = _pair_head(gp, r, half)
                operm += [h * 64 + d for d in range(64)]
        for part in range(2):
            for half in range(2):
                g = 2 * gp + half
                kperm += [g * 64 + part * 32 + i for i in range(32)]
    return np.array(qperm), np.array(kperm), np.array(operm)


_QPERM, _KPERM, _OPERM = _gqa_perms()
_POS_HEAD = np.array([_pair_head(gp, r, half) for gp in range(2) for r in range(4) for half in range(2)])


def _mla_perms():
    uq, kr = [], []
    for hg in range(4):
        for a in range(4):
            uq += [(4 * hg + a) * 96 + d for d in range(64)]
        for part in range(2):
            for a in range(4):
                uq += [(4 * hg + a) * 96 + 64 + part * 16 + i for i in range(16)]
    for part in range(2):
        for _ in range(4):
            kr += [part * 16 + i for i in range(16)]
    kn = [h * 128 + d for h in range(16) for d in range(64)]
    vv = [h * 128 + 64 + d for h in range(16) for d in range(64)]
    return np.array(uq), np.array(kr), np.array(kn), np.array(vv)


_UQPERM, _KRPERM, _KNPERM, _VPERM = _mla_perms()


def _rope_lane_tables(half):
    inv = ROPE_THETA ** (-jnp.arange(half, dtype=F32) / half)
    lane = np.arange(LANES)
    inv_l = inv[lane % half][None, :]
    sign = np.where(lane < 64, -1.0, 1.0).astype(np.float32)[None, :]
    return inv_l, jnp.asarray(sign)


def _rope_table_body(pos_ref, inv_ref, sign_ref, cos_ref, sin_ref):
    ang = pos_ref[...].astype(F32) * inv_ref[...]
    cos_ref[...] = jnp.cos(ang)
    sin_ref[...] = jnp.sin(ang) * sign_ref[...]


def _rope_tables(pos, half):
    n = pos.shape[0]
    tr = min(n, 1024)
    inv_l, sign = _rope_lane_tables(half)
    row = pl.BlockSpec((tr, LANES), lambda i: (i, 0))
    cst = pl.BlockSpec((1, LANES), lambda i: (0, 0))
    return pl.pallas_call(
        _rope_table_body,
        grid=(n // tr,),
        in_specs=[pl.BlockSpec((tr, 1), lambda i: (i, 0)), cst, cst],
        out_specs=[row, row],
        out_shape=[jax.ShapeDtypeStruct((n, LANES), F32)] * 2,
        compiler_params=_cparams(("parallel",)),
        name="rope_tables",
    )(pos.reshape(n, 1), inv_l, sign)


def _rope_cols(v, cos, sin):
    return v * cos + pltpu.roll(v, 64, 1) * sin


def _proj_gqa_body(n_kv, with_cmp, x_ref, w_ref, cos_ref, sin_ref, *outs):
    xb = x_ref[0].astype(BF16)
    cos = cos_ref[0]
    sin = sin_ref[0]

    def mm(c0, width):
        return jnp.dot(xb, w_ref[:, c0:c0 + width], preferred_element_type=F32)

    q_ref, z_ref = outs[0], outs[1]
    for c in range(0, MIX_WIDTH, 512):
        acc = mm(c, 512)
        for j in range(0, 512, LANES):
            q_ref[0, :, c + j:c + j + LANES] = _rope_cols(acc[:, j:j + LANES], cos, sin).astype(BF16)
    for c in range(0, MIX_WIDTH, 512):
        z_ref[0, :, c:c + 512] = mm(MIX_WIDTH + c, 512).astype(BF16)
    col = 2 * MIX_WIDTH
    o = 2
    for _ in range(n_kv):
        acc = mm(col, 2 * KV_WIDTH)
        for j in range(0, KV_WIDTH, LANES):
            outs[o][0, :, j:j + LANES] = _rope_cols(acc[:, j:j + LANES], cos, sin).astype(BF16)
        outs[o + 1][0] = acc[:, KV_WIDTH:].T.astype(BF16)
        col += 2 * KV_WIDTH
        o += 2
    if with_cmp:
        acc = mm(col, 2 * KV_WIDTH)
        for t in range(2):
            for gp in range(2):
                c0 = t * KV_WIDTH + gp * LANES
                outs[o + t][0, gp] = acc[:, c0:c0 + LANES]
        col += 2 * KV_WIDTH
        outs[o + 2][0] = jax.nn.sigmoid(mm(col, 2 * LANES))


def _proj_gqa(x, w, cos, sin, n_kv, with_cmp, tm=256):
    B, S, _ = x.shape
    n = w.shape[1]
    tok = lambda width: pl.BlockSpec((1, tm, width), lambda b, i: (b, i, 0))
    out_specs = [tok(MIX_WIDTH), tok(MIX_WIDTH)]
    out_shape = [jax.ShapeDtypeStruct((B, S, MIX_WIDTH), BF16)] * 2
    for _ in range(n_kv):
        out_specs += [tok(KV_WIDTH), pl.BlockSpec((1, KV_WIDTH, tm), lambda b, i: (b, 0, i))]
        out_shape += [jax.ShapeDtypeStruct((B, S, KV_WIDTH), BF16), jax.ShapeDtypeStruct((B, KV_WIDTH, S), BF16)]
    if with_cmp:
        raw = pl.BlockSpec((1, 2, tm, LANES), lambda b, i: (b, 0, i, 0))
        out_specs += [raw, raw, tok(2 * LANES)]
        out_shape += [jax.ShapeDtypeStruct((B, 2, S, LANES), F32)] * 2
        out_shape += [jax.ShapeDtypeStruct((B, S, 2 * LANES), F32)]
    return pl.pallas_call(
        functools.partial(_proj_gqa_body, n_kv, with_cmp),
        grid=(B, S // tm),
        in_specs=[tok(D_MODEL), pl.BlockSpec((D_MODEL, n), lambda b, i: (0, 0)), tok(LANES), tok(LANES)],
        out_specs=out_specs,
        out_shape=out_shape,
        compiler_params=_cparams(("parallel", "parallel")),
        name="proj_gqa",
    )(x, w, cos, sin)


def _rms(v, g):
    return v * lax.rsqrt(jnp.mean(v * v, -1, keepdims=True) + RMS_EPS) * g


def _proj_mla_body(x_ref, w1_ref, qn_ref, kvn_ref, wuq_ref, wkv_ref, cos_ref, sin_ref,
                   q_ref, k_ref, v_ref, kr_ref, z_ref):
    xb = x_ref[0].astype(BF16)
    cos = cos_ref[0]
    sin = sin_ref[0]
    lat = MLA_Q_LORA + MLA_KV_LORA
    c = jnp.dot(xb, w1_ref[:, :lat + LANES], preferred_element_type=F32)
    cq = _rms(c[:, :MLA_Q_LORA], qn_ref[...]).astype(BF16)
    ckv = _rms(c[:, MLA_Q_LORA:lat], kvn_ref[...]).astype(BF16)
    kr_ref[0] = _rope_cols(c[:, lat:lat + LANES], cos, sin).astype(BF16)
    scale = (MLA_NOPE + MLA_ROPE) ** -0.5
    for hg in range(4):
        c0 = hg * 384
        q = jnp.dot(cq, wuq_ref[:, c0:c0 + 384], preferred_element_type=F32) * scale
        q_ref[0, :, c0:c0 + 256] = q[:, :256].astype(BF16)
        q_ref[0, :, c0 + 256:c0 + 384] = _rope_cols(q[:, 256:], cos, sin).astype(BF16)
    for c0 in range(0, MIX_WIDTH, 512):
        k_ref[0, :, c0:c0 + 512] = jnp.dot(ckv, wkv_ref[:, c0:c0 + 512], preferred_element_type=F32).astype(BF16)
        v_ref[0, c0:c0 + 512, :] = jnp.dot(ckv, wkv_ref[:, MIX_WIDTH + c0:MIX_WIDTH + c0 + 512],
                                            preferred_element_type=F32).T.astype(BF16)
        z_ref[0, :, c0:c0 + 512] = jnp.dot(xb, w1_ref[:, lat + LANES + c0:lat + LANES + c0 + 512],
                                            preferred_element_type=F32).astype(BF16)


def _proj_mla(x, w1, qn, kvn, wuq, wkv, cos, sin, tm=256):
    B, S, _ = x.shape
    tok = lambda width: pl.BlockSpec((1, tm, width), lambda b, i: (b, i, 0))
    full = lambda a: pl.BlockSpec(a.shape, lambda b, i: (0,) * a.ndim)
    widths = [16 * 96, MIX_WIDTH, MIX_WIDTH, LANES, MIX_WIDTH]
    out_specs = [tok(w) for w in widths]
    out_shape = [jax.ShapeDtypeStruct((B, S, w), BF16) for w in widths]
    out_specs[2] = pl.BlockSpec((1, MIX_WIDTH, tm), lambda b, i: (b, 0, i))
    out_shape[2] = jax.ShapeDtypeStruct((B, MIX_WIDTH, S), BF16)
    return pl.pallas_call(
        _proj_mla_body,
        grid=(B, S // tm),
        in_specs=[tok(D_MODEL), full(w1), full(qn), full(kvn), full(wuq), full(wkv), tok(LANES), tok(LANES)],
        out_specs=out_specs,
        out_shape=out_shape,
        compiler_params=_cparams(("parallel", "parallel")),
        name="proj_mla",
    )(x, w1, qn, kvn, wuq, wkv, cos, sin)


_NT = (((1,), (1,)), ((), ()))


def _lane_iota():
    return lax.broadcasted_iota(jnp.int32, (1, LANES), 1)


def _first_half_rows():
    return lax.broadcasted_iota(jnp.int32, (LANES, 1), 0) < 64


def _pair_lhs(q_cols, extra_a=None, extra_b=None):
    m_a = ((_lane_iota() >> 5) & 1) == 0
    a_rows, b_rows = [], []
    for qc in q_cols:
        a = jnp.where(m_a, qc, 0)
        b = jnp.where(m_a, 0, qc)
        if extra_a is not None:
            a = jnp.concatenate([a, extra_a], 1)
            b = jnp.concatenate([b, extra_b], 1)
        a_rows.append(a)
        b_rows.append(b)
    return jnp.concatenate(a_rows + b_rows, 0)


def _online_step(lhs, kt, vt, mask, state):
    m, l, acc = state
    half = lhs.shape[0] // 2
    s = lax.dot_general(kt, lhs, _NT, preferred_element_type=F32)
    if mask is not None:
        s = jnp.where(mask, s, NEG_INF)
    m_new = jnp.maximum(m, jnp.max(s, 0, keepdims=True))
    alpha = jnp.exp(m - m_new)
    p = jnp.exp(s - m_new)
    l = alpha * l + jnp.sum(p, 0, keepdims=True)
    pb = p.astype(BF16)
    top = _first_half_rows()
    pv = (jnp.dot(jnp.where(top, vt, 0), pb[:, :half], preferred_element_type=F32)
          + jnp.dot(jnp.where(top, 0, vt), pb[:, half:], preferred_element_type=F32))
    acc = acc * jnp.where(top, alpha[:, :half], alpha[:, half:]) + pv
    return m_new, l, acc


def _finish(state, gate_rows):
    _, l, acc = state
    half = acc.shape[1]
    inv = 1.0 / l
    if gate_rows is not None:
        inv = inv * gate_rows
    return acc * jnp.where(_first_half_rows(), inv[:, :half], inv[:, half:])


def _gate_rows(gate_t, cols):
    return jnp.concatenate([gate_t[c:c + 1, :] for c in cols], 1)


def _flash_gqa_body(cfg, *refs):
    tq, S, mode, window, use_sel, use_sink, gate_base, tk = cfg
    refs = list(refs)
    q_ref, k_ref, vt_ref = refs[:3]
    nxt = 3
    sb_ref = gate_ref = sink_ref = None
    if use_sel:
        sb_ref = refs[nxt]; nxt += 1
    if gate_base is not None:
        gate_ref = refs[nxt]; nxt += 1
    if use_sink:
        sink_ref = refs[nxt]; nxt += 1
    o_ref = refs[nxt]

    q0 = pl.program_id(1) * tq
    rows = 8 * tq
    tpos = q0 + (lax.broadcasted_iota(jnp.int32, (1, rows), 1) & (tq - 1))

    for gp in range(2):
        q_cols = [q_ref[0, :, (4 * gp + r) * LANES:(4 * gp + r + 1) * LANES] for r in range(4)]
        if use_sel:
            lhs = _pair_lhs(q_cols, sb_ref[0, :, 2 * gp * LANES:(2 * gp + 1) * LANES],
                            sb_ref[0, :, (2 * gp + 1) * LANES:(2 * gp + 2) * LANES])
        else:
            lhs = _pair_lhs(q_cols)

        if use_sink:
            m0 = jnp.concatenate([jnp.full((1, tq), sink_ref[_pair_head(gp, r, half)], F32)
                                  for half in range(2) for r in range(4)], 1)
            l0 = jnp.ones((1, rows), F32)
        else:
            m0 = jnp.full((1, rows), NEG_INF, F32)
            l0 = jnp.zeros((1, rows), F32)
        state = (m0, l0, jnp.zeros((LANES, rows // 2), F32))

        def step(start, state, masked, gp=gp, lhs=lhs):
            start = pl.multiple_of(start, BAND_BLOCK)
            kt = k_ref[0, pl.ds(start, tk), gp * LANES:(gp + 1) * LANES]
            vt = vt_ref[0, gp * LANES:(gp + 1) * LANES, pl.ds(start, tk)]
            mask = None
            if masked:
                kpos = start + lax.broadcasted_iota(jnp.int32, (tk, 1), 0)
                mask = kpos <= tpos
                if mode == "band":
                    mask = mask & (kpos > tpos - window)
            if use_sel:
                blk = (start + lax.broadcasted_iota(jnp.int32, (tk, LANES), 0)) >> 6
                onehot = (blk == lax.broadcasted_iota(jnp.int32, (tk, LANES), 1)).astype(BF16)
                kt = jnp.concatenate([kt, onehot], 1)
            return _online_step(lhs, kt, vt, mask, state)

        if mode == "band":
            state = step(jnp.clip(q0 - (tk - tq), 0, S - tk), state, True)
        else:
            n_full = q0 // tk
            state = lax.fori_loop(0, n_full, lambda j, st: step(j * tk, st, False), state)
            state = step(n_full * tk, state, True)

        gate_rows = None
        if gate_base is not None:
            gate_t = gate_ref[0, :, gp * LANES:(gp + 1) * LANES].T
            gate_rows = _gate_rows(gate_t, [gate_base + 2 * r for r in range(4)]
                                   + [gate_base + 2 * r + 1 for r in range(4)])
        out_t = _finish(state, gate_rows)
        for r in range(4):
            o_ref[0, :, (4 * gp + r) * LANES:(4 * gp + r + 1) * LANES] = (
                out_t[:, r * tq:(r + 1) * tq].T.astype(o_ref.dtype))


def _flash_gqa(q, k, vt, *, mode, window=None, sel=None, gate=None, gate_base=None, sinks=None, tq=128, tk=512):
    B, S, _ = q.shape
    if mode == "band":
        pad = -(-(window - 1) // BAND_BLOCK) * BAND_BLOCK
        tk = pad + tq
    cfg = (tq, S, mode, window, sel is not None, sinks is not None, gate_base if gate is not None else None, tk)
    tok = lambda width: pl.BlockSpec((1, tq, width), lambda b, i: (b, i, 0))
    in_specs = [tok(MIX_WIDTH), pl.BlockSpec((1, S, KV_WIDTH), lambda b, i: (b, 0, 0)),
                pl.BlockSpec((1, KV_WIDTH, S), lambda b, i: (b, 0, 0))]
    args = [q, k, vt]
    if sel is not None:
        in_specs.append(tok(4 * LANES))
        args.append(sel)
    if gate is not None:
        in_specs.append(tok(2 * LANES))
        args.append(gate)
    if sinks is not None:
        in_specs.append(pl.BlockSpec(memory_space=pltpu.SMEM))
        args.append(sinks)
    return pl.pallas_call(
        functools.partial(_flash_gqa_body, cfg),
        grid=(B, S // tq),
        in_specs=in_specs,
        out_specs=tok(MIX_WIDTH),
        out_shape=jax.ShapeDtypeStruct((B, S, MIX_WIDTH), BF16),
        compiler_params=_cparams(("parallel", "arbitrary")),
        name=f"attn_{mode}{'_sel' if sel is not None else ''}{window or ''}",
    )(*args)


def _flash_mla_body(cfg, q_ref, k_ref, vt_ref, kr_ref, o_ref):
    tq, S, tk = cfg
    q0 = pl.program_id(2) * tq
    lane = _lane_iota()
    lo = lane < 64
    slot = (lane >> 4) & 3
    qr = q_ref[0, :, 256:384]
    lhs = []
    for pair in range(2):
        qn = q_ref[0, :, pair * LANES:(pair + 1) * LANES]
        a = jnp.concatenate([jnp.where(lo, qn, 0), jnp.where(slot == 2 * pair, qr, 0)], 1)
        b = jnp.concatenate([jnp.where(lo, 0, qn), jnp.where(slot == 2 * pair + 1, qr, 0)], 1)
        lhs.append(jnp.concatenate([a, b], 0))
    rows = 2 * tq
    tpos = q0 + (lax.broadcasted_iota(jnp.int32, (1, rows), 1) & (tq - 1))

    def init():
        return (jnp.full((1, rows), NEG_INF, F32), jnp.zeros((1, rows), F32), jnp.zeros((LANES, tq), F32))

    def step(start, states, masked):
        start = pl.multiple_of(start, BAND_BLOCK)
        kr = kr_ref[0, pl.ds(start, tk), :]
        mask = None
        if masked:
            mask = (start + lax.broadcasted_iota(jnp.int32, (tk, 1), 0)) <= tpos
        new = []
        for pair in range(2):
            kt = jnp.concatenate([k_ref[0, pl.ds(start, tk), pair * LANES:(pair + 1) * LANES], kr], 1)
            vt = vt_ref[0, pair * LANES:(pair + 1) * LANES, pl.ds(start, tk)]
            new.append(_online_step(lhs[pair], kt, vt, mask, states[pair]))
        return tuple(new)

    n_full = q0 // tk
    states = lax.fori_loop(0, n_full, lambda j, st: step(j * tk, st, False), (init(), init()))
    states = step(n_full * tk, states, True)
    for pair in range(2):
        o_ref[0, :, pair * LANES:(pair + 1) * LANES] = _finish(states[pair], None).T.astype(o_ref.dtype)


def _flash_mla(q, k, vt, kr, tq=256, tk=512):
    B, S, _ = q.shape
    return pl.pallas_call(
        functools.partial(_flash_mla_body, (tq, S, tk)),
        grid=(B, 4, S // tq),
        in_specs=[pl.BlockSpec((1, tq, 384), lambda b, g, i: (b, i, g)),
                  pl.BlockSpec((1, S, 256), lambda b, g, i: (b, 0, g)),
                  pl.BlockSpec((1, 256, S), lambda b, g, i: (b, g, 0)),
                  pl.BlockSpec((1, S, LANES), lambda b, g, i: (b, 0, 0))],
        out_specs=pl.BlockSpec((1, tq, 256), lambda b, g, i: (b, i, g)),
        out_shape=jax.ShapeDtypeStruct((B, S, MIX_WIDTH), BF16),
        compiler_params=_cparams(("parallel", "parallel", "arbitrary")),
        name="attn_mla",
    )(q, k, vt, kr)


def _compress_body(kraw_ref, vraw_ref, posa_ref, posb_ref, wk1a_ref, wk1b_ref, wk2_ref,
                   wv1a_ref, wv1b_ref, wv2_ref, cos_ref, sin_ref, kc_ref, vct_ref):
    n = kraw_ref.shape[2]

    def phi(raw, w1a, w1b, w2):
        a = jnp.dot((raw + posa_ref[...]).astype(BF16), w1a[...], preferred_element_type=F32)
        b = jnp.dot((raw + posb_ref[...]).astype(BF16), w1b[...], preferred_element_type=F32)
        h = a + pltpu.roll(b, n - 1, 0)
        return jnp.dot(jax.nn.silu(h).astype(BF16), w2[...], preferred_element_type=F32)

    kc = phi(kraw_ref[0, 0], wk1a_ref, wk1b_ref, wk2_ref)
    kc_ref[0, 0] = _rope_cols(kc, cos_ref[0], sin_ref[0]).astype(BF16)
    vct_ref[0, 0] = phi(vraw_ref[0, 0], wv1a_ref, wv1b_ref, wv2_ref).T.astype(BF16)


def _compress(kraw, vraw, posa, posb, wk, wv, cos_c, sin_c):
    B = kraw.shape[0]
    n = kraw.shape[2]
    raw = pl.BlockSpec((1, 1, n, kraw.shape[3]), lambda b, g: (b, g, 0, 0))
    full = lambda a: pl.BlockSpec(a.shape, lambda b, g: (0,) * a.ndim)
    tab = pl.BlockSpec((1, n, LANES), lambda b, g: (b, 0, 0))
    return pl.pallas_call(
        _compress_body,
        grid=(B, 2),
        in_specs=[raw, raw, full(posa), full(posb)] + [full(w) for w in wk] + [full(w) for w in wv] + [tab, tab],
        out_specs=[pl.BlockSpec((1, 1, n, LANES), lambda b, g: (b, g, 0, 0)),
                   pl.BlockSpec((1, 1, LANES, n), lambda b, g: (b, g, 0, 0))],
        out_shape=[jax.ShapeDtypeStruct((B, 2, n, LANES), BF16), jax.ShapeDtypeStruct((B, 2, LANES, n), BF16)],
        compiler_params=_cparams(("parallel", "parallel")),
        name="nsa_compress",
    )(kraw, vraw, posa, posb, *wk, *wv, cos_c, sin_c)


def _split3(p):
    hi = p.astype(BF16)
    r1 = p - hi.astype(F32)
    mid = r1.astype(BF16)
    lo = (r1 - mid.astype(F32)).astype(BF16)
    return hi, mid, lo


def _cmp_select_body(cfg, q_ref, kc_ref, vct_ref, ovt_ref, gate_ref, o_ref, sb_ref):
    tq, n_cmp_pad, n_blk = cfg
    q0 = pl.program_id(1) * tq
    rows = 8 * tq
    half_rows = rows // 2
    tpos = q0 + (lax.broadcasted_iota(jnp.int32, (1, rows), 1) & (tq - 1))
    blk_end = lax.broadcasted_iota(jnp.int32, (n_cmp_pad, 1), 0) * CMP_STRIDE + (CMP_BLOCK - 1)
    cmask = blk_end <= tpos
    top = _first_half_rows()
    jrow = lax.broadcasted_iota(jnp.int32, (n_blk, 1), 0)
    cur = (q0 + lax.broadcasted_iota(jnp.int32, (1, tq), 1)) >> 6
    forced = (jrow == 0) | (jrow == cur) | (jrow == cur - 1)
    ovt = ovt_ref[...]

    for gp in range(2):
        lhs = _pair_lhs([q_ref[0, :, (4 * gp + r) * LANES:(4 * gp + r + 1) * LANES] for r in range(4)])
        s = lax.dot_general(kc_ref[0, gp], lhs, _NT, preferred_element_type=F32)
        s = jnp.where(cmask, s, NEG_INF)
        m = jnp.max(s, 0, keepdims=True)
        p = jnp.where(cmask, jnp.exp(s - m), 0.0)
        l = jnp.sum(p, 0, keepdims=True)
        pc = p * jnp.where(l > 0.0, 1.0 / l, 0.0)

        vct = vct_ref[0, gp]
        pb = pc.astype(BF16)
        pv = (jnp.dot(jnp.where(top, vct, 0), pb[:, :half_rows], preferred_element_type=F32)
              + jnp.dot(jnp.where(top, 0, vct), pb[:, half_rows:], preferred_element_type=F32))
        gate_t = gate_ref[0, :, gp * LANES:(gp + 1) * LANES].T
        ga = _gate_rows(gate_t, [2 * r for r in range(4)])
        gb = _gate_rows(gate_t, [2 * r + 1 for r in range(4)])
        out_t = pv * jnp.where(top, ga, gb)
        for r in range(4):
            o_ref[0, :, (4 * gp + r) * LANES:(4 * gp + r + 1) * LANES] = (
                out_t[:, r * tq:(r + 1) * tq].T.astype(o_ref.dtype))

        for half in range(2):
            c0 = 4 * half * tq
            psum = pc[:, c0:c0 + tq] + pc[:, c0 + tq:c0 + 2 * tq] + pc[:, c0 + 2 * tq:c0 + 3 * tq] + pc[:, c0 + 3 * tq:c0 + 4 * tq]
            imp = sum(jnp.dot(ovt, part, preferred_element_type=F32) for part in _split3(psum))[:n_blk]
            score = jnp.where(forced, SEL_FORCE, jnp.where(jrow <= cur, imp, -SEL_FORCE))
            rank = jnp.zeros((n_blk, tq), jnp.int32)
            for j in range(n_blk):
                row = score[j:j + 1, :]
                beats = (row > score) | ((row == score) & (jrow > j))
                rank = rank + beats.astype(jnp.int32)
            bias = jnp.where(rank < min(N_SEL, n_blk), 0.0, NEG_INF)
            bias = jnp.concatenate([bias, jnp.zeros((LANES - n_blk, tq), F32)], 0)
            sb_ref[0, :, (2 * gp + half) * LANES:(2 * gp + half + 1) * LANES] = bias.T.astype(BF16)


def _cmp_select(q, kc, vct, ovt, gate, tq=128):
    B, S, _ = q.shape
    n_cmp_pad = kc.shape[2]
    n_blk = S // SEL_BLOCK
    tok = lambda width: pl.BlockSpec((1, tq, width), lambda b, i: (b, i, 0))
    return pl.pallas_call(
        functools.partial(_cmp_select_body, (tq, n_cmp_pad, n_blk)),
        grid=(B, S // tq),
        in_specs=[tok(MIX_WIDTH), pl.BlockSpec((1, 2, n_cmp_pad, LANES), lambda b, i: (b, 0, 0, 0)),
                  pl.BlockSpec((1, 2, LANES, n_cmp_pad), lambda b, i: (b, 0, 0, 0)),
                  pl.BlockSpec(ovt.shape, lambda b, i: (0, 0)), tok(2 * LANES)],
        out_specs=[tok(MIX_WIDTH), tok(4 * LANES)],
        out_shape=[jax.ShapeDtypeStruct((B, S, MIX_WIDTH), BF16), jax.ShapeDtypeStruct((B, S, 4 * LANES), BF16)],
        compiler_params=_cparams(("parallel", "parallel")),
        name="nsa_cmp_select",
    )(q, kc, vct, ovt, gate)


def _out_body(n_o, *refs):
    o_refs = refs[:n_o]
    z_ref, x_ref, p_ref, wo_ref, g_ref, b_ref, wg_ref, wp_ref, out_ref = refs[n_o:]
    o = o_refs[0][0].astype(F32)
    for r in o_refs[1:]:
        o = o + r[0].astype(F32)
    gated = (o * jax.nn.silu(z_ref[0].astype(F32))).astype(BF16)
    y = jnp.dot(gated, wo_ref[...], preferred_element_type=F32)
    u = DEEPNORM_ALPHA * x_ref[0] + y
    mu = jnp.mean(u, -1, keepdims=True)
    var = jnp.mean(jnp.square(u - mu), -1, keepdims=True)
    xn = (u - mu) * lax.rsqrt(var + LN_EPS) * g_ref[...] + b_ref[...]
    gate = jax.nn.sigmoid(jnp.dot(xn.astype(BF16), wg_ref[...], preferred_element_type=F32))
    emb = jnp.dot(p_ref[0].astype(BF16), wp_ref[...], preferred_element_type=F32)
    out_ref[0] = xn + gate * emb


def _out_block(o_list, z, x, p, wo, g, b, wg, wp, tm=256):
    B, S, _ = x.shape
    tok = lambda width: pl.BlockSpec((1, tm, width), lambda bb, i: (bb, i, 0))
    full = lambda a: pl.BlockSpec(a.shape, lambda bb, i: (0,) * a.ndim)
    n_o = len(o_list)
    return pl.pallas_call(
        functools.partial(_out_body, n_o),
        grid=(B, S // tm),
        in_specs=[tok(MIX_WIDTH)] * n_o + [tok(MIX_WIDTH), tok(D_MODEL), tok(PLE_DIM),
                                           full(wo), full(g), full(b), full(wg), full(wp)],
        out_specs=tok(D_MODEL),
        out_shape=jax.ShapeDtypeStruct((B, S, D_MODEL), F32),
        compiler_params=_cparams(("parallel", "parallel")),
        name="out_block",
    )(*o_list, z, x, p, wo, g, b, wg, wp)


def _swa_weights(w_in):
    q, k, v, z = jnp.split(w_in, [MIX_WIDTH, MIX_WIDTH + KV_WIDTH, MIX_WIDTH + 2 * KV_WIDTH], axis=1)
    q = q[:, _QPERM] * (HEAD_DIM ** -0.5)
    return jnp.concatenate([q, z[:, _OPERM], k[:, _KPERM], v], 1).astype(BF16)


def _nsa_weights(w_in):
    cuts = np.cumsum([MIX_WIDTH] + [KV_WIDTH] * 6 + [3 * N_HEADS]).tolist()
    q, k_c, v_c, k_s, v_s, k_w, v_w, gl, z = jnp.split(w_in, cuts, axis=1)
    q = q[:, _QPERM] * (HEAD_DIM ** -0.5)
    gl = jnp.concatenate(
        [jnp.pad(gl[:, np.array([br * N_HEADS + _POS_HEAD[gp * 8 + c] for br in range(3) for c in range(8)])],
                 ((0, 0), (0, LANES - 24))) for gp in range(2)], 1)
    return jnp.concatenate([q, z[:, _OPERM], k_s[:, _KPERM], v_s, k_w[:, _KPERM], v_w, k_c, v_c, gl], 1).astype(BF16)


def _phi_weights(w1, w2, rope_layout):
    h = w1.shape[1]
    w1 = w1.reshape(2, CMP_STRIDE, HEAD_DIM, h)
    eye = jnp.eye(2, dtype=w1.dtype)
    w1a, w1b = (jnp.einsum("ldj,gh->lgdhj", w1[t], eye).reshape(CMP_STRIDE * 2 * HEAD_DIM, 2 * h) for t in range(2))
    if rope_layout:
        w2 = jnp.einsum("jpi,gh->gjphi", w2.reshape(h, 2, 32), eye).reshape(2 * h, LANES)
    else:
        w2 = jnp.einsum("jd,gh->gjhd", w2, eye).reshape(2 * h, LANES)
    return w1a.astype(BF16), w1b.astype(BF16), w2.astype(BF16)


def _overlap_matrix_t(n_cmp, n_cmp_pad, n_blk):
    cstart = np.arange(n_cmp_pad) * CMP_STRIDE
    j = np.arange(LANES)
    ov = (cstart[None, :] < (j[:, None] + 1) * SEL_BLOCK) & (cstart[None, :] + CMP_BLOCK > j[:, None] * SEL_BLOCK)
    ov &= (np.arange(n_cmp_pad)[None, :] < n_cmp) & (j[:, None] < n_blk)
    return jnp.asarray(ov.astype(np.float32), dtype=BF16)


def _layer_swa(x, tabs64, w_in, sinks):
    q, z, k, vt = _proj_gqa(x, _swa_weights(w_in), *tabs64, n_kv=1, with_cmp=False)
    o = _flash_gqa(q, k, vt, mode="band", window=SWA_WINDOW, sinks=sinks.astype(F32))
    return [o], z, _OPERM


def _layer_mla(x, tabs32, w_in, q_norm, kv_norm, w_uq, w_ukv):
    lat = MLA_Q_LORA + MLA_KV_LORA
    w1 = jnp.concatenate([w_in[:, :lat], w_in[:, lat:lat + MLA_ROPE][:, _KRPERM], w_in[:, lat + MLA_ROPE:]], 1)
    wkv = jnp.concatenate([w_ukv[:, _KNPERM], w_ukv[:, _VPERM]], 1)
    q, k, vt, kr, z = _proj_mla(x, w1.astype(BF16), q_norm[None, :], kv_norm[None, :],
                                w_uq[:, _UQPERM].astype(BF16), wkv.astype(BF16), *tabs32)
    return [_flash_mla(q, k, vt, kr)], z, None


def _layer_nsa(x, tabs64, positions, w_in, cmp_pos, phi_k1, phi_k2, phi_v1, phi_v2):
    B, S, _ = x.shape
    q, z, ks, vst, kw, vwt, kraw, vraw, gate = _proj_gqa(x, _nsa_weights(w_in), *tabs64, n_kv=2, with_cmp=True)
    n_chunk = S // CMP_STRIDE
    n_cmp = (S - CMP_BLOCK) // CMP_STRIDE + 1
    kraw = kraw.reshape(B, 2, n_chunk, CMP_STRIDE * LANES)
    vraw = vraw.reshape(B, 2, n_chunk, CMP_STRIDE * LANES)
    pos2 = jnp.tile(cmp_pos.reshape(2, CMP_STRIDE, 1, HEAD_DIM), (1, 1, 2, 1)).reshape(2, 1, CMP_STRIDE * LANES)
    end = jnp.minimum(jnp.arange(n_chunk) * CMP_STRIDE + CMP_BLOCK - 1, S - 1)
    cos_c, sin_c = _rope_tables(positions[:, end].reshape(-1), HEAD_DIM // 2)
    kc, vct = _compress(kraw, vraw, pos2[0], pos2[1], _phi_weights(phi_k1, phi_k2, True),
                        _phi_weights(phi_v1, phi_v2, False),
                        cos_c.reshape(B, n_chunk, LANES), sin_c.reshape(B, n_chunk, LANES))
    ovt = _overlap_matrix_t(n_cmp, n_chunk, S // SEL_BLOCK)
    o_cmp, sel = _cmp_select(q, kc, vct, ovt, gate)
    o_slc = _flash_gqa(q, ks, vst, mode="causal", sel=sel, gate=gate, gate_base=8)
    o_win = _flash_gqa(q, kw, vwt, mode="band", window=NSA_WINDOW, gate=gate, gate_base=16)
    return [o_cmp, o_slc, o_win], z, _OPERM


def kernel(x, p, positions, w_out, ln_g, ln_b, pe_gate, pe_proj, l0_w_in, l0_sinks, l1_w_in, l1_q_norm, l1_kv_norm, l1_w_uq, l1_w_ukv, l2_w_in, l2_cmp_pos, l2_phi_k1, l2_phi_k2, l2_phi_v1, l2_phi_v2, l3_w_in, l3_sinks):
    B, S, _ = x.shape
    pos_flat = positions.reshape(-1)
    tabs64 = [t.reshape(B, S, LANES) for t in _rope_tables(pos_flat, HEAD_DIM // 2)]
    tabs32 = [t.reshape(B, S, LANES) for t in _rope_tables(pos_flat, MLA_ROPE // 2)]
    layers = (
        lambda h: _layer_swa(h, tabs64, l0_w_in, l0_sinks),
        lambda h: _layer_mla(h, tabs32, l1_w_in, l1_q_norm, l1_kv_norm, l1_w_uq, l1_w_ukv),
        lambda h: _layer_nsa(h, tabs64, positions, l2_w_in, l2_cmp_pos, l2_phi_k1, l2_phi_k2, l2_phi_v1, l2_phi_v2),
        lambda h: _layer_swa(h, tabs64, l3_w_in, l3_sinks),
    )
    for i, layer in enumerate(layers):
        o_list, z, operm = layer(x)
        wo = w_out[i] if operm is None else w_out[i][operm, :]
        x = _out_block(o_list, z, x, p[i], wo.astype(BF16), ln_g[i][None, :], ln_b[i][None, :],
                       pe_gate[i].astype(BF16), pe_proj[i].astype(BF16))
    return x
```

```python
import functools

import numpy as np
import jax
import jax.numpy as jnp
from jax import lax
from jax.experimental import pallas as pl
from jax.experimental.pallas import tpu as pltpu

F32 = jnp.float32
BF16 = jnp.bfloat16

LANES = 128
VMEM_LIMIT = 56 * 1024 * 1024

D_MODEL = 1024
DEPTH = 4
HEAD_DIM = 64
N_HEADS = 16
MIX_WIDTH = 1024
ROPE_THETA = 10000.0
NEG_INF = -1e30
PLE_DIM = 256
DEEPNORM_ALPHA = (2 * DEPTH) ** 0.25
LN_EPS = 1e-5
RMS_EPS = 1e-6

SWA_WINDOW = 128
KV_GROUPS = 4
KV_WIDTH = KV_GROUPS * HEAD_DIM

MLA_Q_LORA = 384
MLA_KV_LORA = 256
MLA_NOPE = 64
MLA_ROPE = 32
MLA_V = 64

CMP_BLOCK = 32
CMP_STRIDE = 16
SEL_BLOCK = 64
N_SEL = 8
SEL_FORCE = 1e9
NSA_WINDOW = 512
PHI_HIDDEN = 256

BAND_BLOCK = 128
LOG2E = 1.4426950408889634


def _cparams(sem, flags=None):
    return pltpu.CompilerParams(dimension_semantics=sem, vmem_limit_bytes=VMEM_LIMIT, flags=flags)


_ATTN_FLAGS = None


def _pair_head(gp, r, half):
    return (2 * gp + half) * 4 + r


def _gqa_perms():
    qperm, kperm, operm = [], [], []
    for gp in range(2):
        for r in range(4):
            for part in range(2):
                for half in range(2):
                    h = _pair_head(gp, r, half)
                    qperm += [h * 64 + part * 32 + i for i in range(32)]
            for half in range(2):
                h = _pair_head(gp, r, half)
                operm += [h * 64 + d for d in range(64)]
        for part in range(2):
            for half in range(2):
                g = 2 * gp + half
                kperm += [g * 64 + part * 32 + i for i in range(32)]
    return np.array(qperm), np.array(kperm), np.array(operm)


_QPERM, _KPERM, _OPERM = _gqa_perms()
_POS_HEAD = np.array([_pair_head(gp, r, half) for gp in range(2) for r in range(4) for half in range(2)])


def _mla_perms():
    uq, kr = [], []
    for hg in range(4):
        for a in range(4):
            uq += [(4 * hg + a) * 96 + d for d in range(64)]
        for part in range(2):
            for a in range(4):
                uq += [(4 * hg + a) * 96 + 64 + part * 16 + i for i in range(16)]
    for part in range(2):
        for _ in range(4):
            kr += [part * 16 + i for i in range(16)]
    kn = [h * 128 + d for h in range(16) for d in range(64)]
    vv = [h * 128 + 64 + d for h in range(16) for d in range(64)]
    return np.array(uq), np.array(kr), np.array(kn), np.array(vv)


_UQPERM, _KRPERM, _KNPERM, _VPERM = _mla_perms()


def _rope_lane_tables(half):
    inv = ROPE_THETA ** (-jnp.arange(half, dtype=F32) / half)
    lane = np.arange(LANES)
    inv_l = inv[lane % half][None, :]
    sign = np.where(lane < 64, -1.0, 1.0).astype(np.float32)[None, :]
    return inv_l, jnp.asarray(sign)


def _rope_table_body(pos_ref, inv_ref, sign_ref, cos_ref, sin_ref):
    ang = pos_ref[...].astype(F32) * inv_ref[...]
    cos_ref[...] = jnp.cos(ang)
    sin_ref[...] = jnp.sin(ang) * sign_ref[...]


def _rope_tables(pos, half):
    n = pos.shape[0]
    tr = min(n, 1024)
    inv_l, sign = _rope_lane_tables(half)
    row = pl.BlockSpec((tr, LANES), lambda i: (i, 0))
    cst = pl.BlockSpec((1, LANES), lambda i: (0, 0))
    return pl.pallas_call(
        _rope_table_body,
        grid=(n // tr,),
        in_specs=[pl.BlockSpec((tr, 1), lambda i: (i, 0)), cst, cst],
        out_specs=[row, row],
        out_shape=[jax.ShapeDtypeStruct((n, LANES), F32)] * 2,
        compiler_params=_cparams(("parallel",)),
        name="rope_tables",
    )(pos.reshape(n, 1), inv_l, sign)


def _rope_cols(v, cos, sin):
    return v * cos + pltpu.roll(v, 64, 1) * sin


def _proj_gqa_body(n_kv, with_cmp, x_ref, w_ref, cos_ref, sin_ref, *outs):
    xb = x_ref[0].astype(BF16)
    cos = cos_ref[0]
    sin = sin_ref[0]

    def mm(c0, width):
        return jnp.dot(xb, w_ref[:, c0:c0 + width], preferred_element_type=F32)

    q_ref, z_ref = outs[0], outs[1]
    for c in range(0, MIX_WIDTH, 512):
        acc = mm(c, 512) * (HEAD_DIM ** -0.5 * LOG2E)
        for j in range(0, 512, LANES):
            q_ref[0, :, c + j:c + j + LANES] = _rope_cols(acc[:, j:j + LANES], cos, sin).astype(BF16)
    for c in range(0, MIX_WIDTH, 512):
        z_ref[0, :, c:c + 512] = mm(MIX_WIDTH + c, 512).astype(BF16)
    col = 2 * MIX_WIDTH
    o = 2
    for _ in range(n_kv):
        acc = mm(col, 2 * KV_WIDTH)
        for j in range(0, KV_WIDTH, LANES):
            outs[o][0, :, j:j + LANES] = _rope_cols(acc[:, j:j + LANES], cos, sin).astype(BF16)
        outs[o + 1][0] = acc[:, KV_WIDTH:].T.astype(BF16)
        col += 2 * KV_WIDTH
        o += 2
    if with_cmp:
        acc = mm(col, 2 * KV_WIDTH)
        for t in range(2):
            for gp in range(2):
                c0 = t * KV_WIDTH + gp * LANES
                outs[o + t][0, gp] = acc[:, c0:c0 + LANES]
        col += 2 * KV_WIDTH
        outs[o + 2][0] = jax.nn.sigmoid(mm(col, 2 * LANES))


def _proj_gqa(x, w, cos, sin, n_kv, with_cmp, tm=512):
    B, S, _ = x.shape
    n = w.shape[1]
    tok = lambda width: pl.BlockSpec((1, tm, width), lambda b, i: (b, i, 0))
    out_specs = [tok(MIX_WIDTH), tok(MIX_WIDTH)]
    out_shape = [jax.ShapeDtypeStruct((B, S, MIX_WIDTH), BF16)] * 2
    for _ in range(n_kv):
        out_specs += [tok(KV_WIDTH), pl.BlockSpec((1, KV_WIDTH, tm), lambda b, i: (b, 0, i))]
        out_shape += [jax.ShapeDtypeStruct((B, S, KV_WIDTH), BF16), jax.ShapeDtypeStruct((B, KV_WIDTH, S), BF16)]
    if with_cmp:
        raw = pl.BlockSpec((1, 2, tm, LANES), lambda b, i: (b, 0, i, 0))
        out_specs += [raw, raw, tok(2 * LANES)]
        out_shape += [jax.ShapeDtypeStruct((B, 2, S, LANES), F32)] * 2
        out_shape += [jax.ShapeDtypeStruct((B, S, 2 * LANES), F32)]
    return pl.pallas_call(
        functools.partial(_proj_gqa_body, n_kv, with_cmp),
        grid=(B, S // tm),
        in_specs=[tok(D_MODEL), pl.BlockSpec((D_MODEL, n), lambda b, i: (0, 0)), tok(LANES), tok(LANES)],
        out_specs=out_specs,
        out_shape=out_shape,
        compiler_params=_cparams(("parallel", "parallel")),
        name="proj_gqa",
    )(x, w, cos, sin)


def _rms(v, g):
    return v * lax.rsqrt(jnp.mean(v * v, -1, keepdims=True) + RMS_EPS) * g


def _proj_mla_body(x_ref, w1_ref, qn_ref, kvn_ref, wuq_ref, wkv_ref, cos_ref, sin_ref,
                   q_ref, k_ref, v_ref, kr_ref, z_ref):
    xb = x_ref[0].astype(BF16)
    cos = cos_ref[0]
    sin = sin_ref[0]
    lat = MLA_Q_LORA + MLA_KV_LORA
    c = jnp.dot(xb, w1_ref[:, :lat + LANES], preferred_element_type=F32)
    cq = _rms(c[:, :MLA_Q_LORA], qn_ref[...]).astype(BF16)
    ckv = _rms(c[:, MLA_Q_LORA:lat], kvn_ref[...]).astype(BF16)
    kr_ref[0] = _rope_cols(c[:, lat:lat + LANES], cos, sin).astype(BF16)
    scale = (MLA_NOPE + MLA_ROPE) ** -0.5 * LOG2E
    for hg in range(4):
        c0 = hg * 384
        q = jnp.dot(cq, wuq_ref[:, c0:c0 + 384], preferred_element_type=F32) * scale
        q_ref[0, :, c0:c0 + 256] = q[:, :256].astype(BF16)
        q_ref[0, :, c0 + 256:c0 + 384] = _rope_cols(q[:, 256:], cos, sin).astype(BF16)
    for c0 in range(0, MIX_WIDTH, 512):
        k_ref[0, :, c0:c0 + 512] = jnp.dot(ckv, wkv_ref[:, c0:c0 + 512], preferred_element_type=F32).astype(BF16)
        v_ref[0, c0:c0 + 512, :] = jnp.dot(ckv, wkv_ref[:, MIX_WIDTH + c0:MIX_WIDTH + c0 + 512],
                                            preferred_element_type=F32).T.astype(BF16)
        z_ref[0, :, c0:c0 + 512] = jnp.dot(xb, w1_ref[:, lat + LANES + c0:lat + LANES + c0 + 512],
                                            preferred_element_type=F32).astype(BF16)


def _proj_mla(x, w1, qn, kvn, wuq, wkv, cos, sin, tm=512):
    B, S, _ = x.shape
    tok = lambda width: pl.BlockSpec((1, tm, width), lambda b, i: (b, i, 0))
    full = lambda a: pl.BlockSpec(a.shape, lambda b, i: (0,) * a.ndim)
    widths = [16 * 96, MIX_WIDTH, MIX_WIDTH, LANES, MIX_WIDTH]
    out_specs = [tok(w) for w in widths]
    out_shape = [jax.ShapeDtypeStruct((B, S, w), BF16) for w in widths]
    out_specs[2] = pl.BlockSpec((1, MIX_WIDTH, tm), lambda b, i: (b, 0, i))
    out_shape[2] = jax.ShapeDtypeStruct((B, MIX_WIDTH, S), BF16)
    return pl.pallas_call(
        _proj_mla_body,
        grid=(B, S // tm),
        in_specs=[tok(D_MODEL), full(w1), full(qn), full(kvn), full(wuq), full(wkv), tok(LANES), tok(LANES)],
        out_specs=out_specs,
        out_shape=out_shape,
        compiler_params=_cparams(("parallel", "parallel")),
        name="proj_mla",
    )(x, w1, qn, kvn, wuq, wkv, cos, sin)


_NT = (((1,), (1,)), ((), ()))


def _lane_iota():
    return lax.broadcasted_iota(jnp.int32, (1, LANES), 1)


def _first_half_rows():
    return lax.broadcasted_iota(jnp.int32, (LANES, 1), 0) < 64


def _pair_lhs(q_cols, extra_a=None, extra_b=None):
    m_a = ((_lane_iota() >> 5) & 1) == 0
    a_rows, b_rows = [], []
    for qc in q_cols:
        a = jnp.where(m_a, qc, 0)
        b = jnp.where(m_a, 0, qc)
        if extra_a is not None:
            a = jnp.concatenate([a, extra_a], 1)
            b = jnp.concatenate([b, extra_b], 1)
        a_rows.append(a)
        b_rows.append(b)
    return jnp.concatenate(a_rows + b_rows, 0)


def _tile_lanes(bias, reps):
    return jnp.concatenate([bias] * reps, 1)


def _online_step(lhs, kt, vt, bias, state):
    m, acc_a, acc_b = state
    half = lhs.shape[0] // 2
    s = lax.dot_general(kt, lhs, _NT, preferred_element_type=F32)
    if bias is not None:
        s = s + bias
    m_new = jnp.maximum(m, jnp.max(s, 0, keepdims=True))
    p = jnp.exp2(s - m_new).astype(BF16)
    top = _first_half_rows()
    pv_a = jnp.dot(jnp.where(top, vt, 1), p[:, :half], preferred_element_type=F32)
    pv_b = jnp.dot(jnp.where(top, 1, vt), p[:, half:], preferred_element_type=F32)
    if acc_a is None:
        return m_new, pv_a, pv_b
    alpha = jnp.exp2(m - m_new)
    return m_new, acc_a * alpha[:, :half] + pv_a, acc_b * alpha[:, half:] + pv_b


def _finish(state, gate_rows):
    _, acc_a, acc_b = state
    half = acc_a.shape[1]
    inv_a = 1.0 / acc_a[64:65, :]
    inv_b = 1.0 / acc_b[0:1, :]
    if gate_rows is not None:
        inv_a = inv_a * gate_rows[:, :half]
        inv_b = inv_b * gate_rows[:, half:]
    return jnp.where(_first_half_rows(), acc_a * inv_a, acc_b * inv_b)


def _gate_rows(gate_t, cols):
    return jnp.concatenate([gate_t[c:c + 1, :] for c in cols], 1)


def _flash_gqa_body(cfg, *refs):
    tq, nsub, n_prev, mode, use_sel, use_sink, gate_base, tk = cfg
    refs = list(refs)
    q_ref, k_ref, vt_ref, bias_ref = refs[:4]
    nxt = 4
    sb_ref = gate_ref = sink_ref = None
    if use_sel:
        sb_ref = refs[nxt]; nxt += 1
    if gate_base is not None:
        gate_ref = refs[nxt]; nxt += 1
    if use_sink:
        sink_ref = refs[nxt]; nxt += 1
    o_ref = refs[nxt]

    step_i = pl.program_id(1)
    rows = 8 * tq
    top = _first_half_rows()
    chains = [(sub, gp) for sub in range(nsub) for gp in range(2)]
    lhs, init = [], []
    for sub, gp in chains:
        tok = slice(sub * tq, (sub + 1) * tq)
        q_cols = [q_ref[0, tok, (4 * gp + r) * LANES:(4 * gp + r + 1) * LANES] for r in range(4)]
        if use_sel:
            lhs.append(_pair_lhs(q_cols, sb_ref[0, tok, 2 * gp * LANES:(2 * gp + 1) * LANES],
                                 sb_ref[0, tok, (2 * gp + 1) * LANES:(2 * gp + 2) * LANES]))
        else:
            lhs.append(_pair_lhs(q_cols))
        if use_sink:
            m0 = jnp.concatenate([jnp.full((1, tq), sink_ref[_pair_head(gp, r, half)] * LOG2E, F32)
                                  for half in range(2) for r in range(4)], 1)
            init.append((m0, jnp.where(top, jnp.zeros((LANES, rows // 2), F32), 1.0),
                         jnp.where(top, jnp.ones((LANES, rows // 2), F32), 0.0)))
        else:
            init.append((jnp.full((1, rows), NEG_INF, F32), None, None))

    def step(c, start, state, bias):
        gp = chains[c][1]
        start = pl.multiple_of(start, BAND_BLOCK)
        kt = k_ref[0, pl.ds(start, tk), gp * LANES:(gp + 1) * LANES]
        vt = vt_ref[0, gp * LANES:(gp + 1) * LANES, pl.ds(start, tk)]
        if use_sel:
            blk = (start + lax.broadcasted_iota(jnp.int32, (tk, LANES), 0)) >> 6
            onehot = (blk == lax.broadcasted_iota(jnp.int32, (tk, LANES), 1)).astype(BF16)
            kt = jnp.concatenate([kt, onehot], 1)
        return _online_step(lhs[c], kt, vt, bias, state)

    if mode == "band":
        states = []
        for c, (sub, gp) in enumerate(chains):
            tile = step_i * nsub + sub
            bias = _tile_lanes(bias_ref[jnp.minimum(tile, n_prev)], 8)
            states.append(step(c, jnp.maximum(tile * tq - (tk - tq), 0), init[c], bias))
    else:
        q0 = step_i * tq
        n_full = q0 // tk
        bias = _tile_lanes(bias_ref[step_i % (tk // tq)], 8)
        states = tuple(step(c, n_full * tk, init[c], bias) for c in range(len(chains)))
        states = lax.fori_loop(
            0, n_full, lambda j, sts: tuple(step(c, j * tk, sts[c], None) for c in range(len(chains))), states)

    for c, (sub, gp) in enumerate(chains):
        tok = slice(sub * tq, (sub + 1) * tq)
        gate_rows = None
        if gate_base is not None:
            gate_t = gate_ref[0, tok, gp * LANES:(gp + 1) * LANES].T
            gate_rows = _gate_rows(gate_t, [gate_base + 2 * r for r in range(4)]
                                   + [gate_base + 2 * r + 1 for r in range(4)])
        out_t = _finish(states[c], gate_rows)
        for r in range(4):
            o_ref[0, tok, (4 * gp + r) * LANES:(4 * gp + r + 1) * LANES] = (
                out_t[:, r * tq:(r + 1) * tq].T.astype(o_ref.dtype))


def _band_bias(window, tq, tk):
    pad = tk - tq
    r = np.arange(tk)[:, None]
    c = np.arange(tq)[None, :]
    out = []
    for v in range(pad // tq + 1):
        d = r - c - (v * tq if v < pad // tq else pad)
        out.append(np.where((d <= 0) & (d > -window), 0.0, NEG_INF))
    return jnp.asarray(np.stack(out), F32)


def _diag_bias(tq, tk):
    r = np.arange(tk)[:, None]
    c = np.arange(tq)[None, :]
    return jnp.asarray(np.stack([np.where(r <= v * tq + c, 0.0, NEG_INF) for v in range(tk // tq)]), F32)


def _flash_gqa(q, k, vt, *, mode, window=None, sel=None, gate=None, gate_base=None, sinks=None,
               tq=128, nsub=1, tk=512):
    B, S, _ = q.shape
    n_prev = 0
    if mode == "band":
        pad = -(-(window - 1) // BAND_BLOCK) * BAND_BLOCK
        tk = pad + tq
        bias = _band_bias(window, tq, tk)
        n_prev = pad // tq
    else:
        assert nsub == 1
        bias = _diag_bias(tq, tk)
    bias_spec = pl.BlockSpec(bias.shape, lambda b, i: (0, 0, 0))
    cfg = (tq, nsub, n_prev, mode, sel is not None, sinks is not None, gate_base if gate is not None else None, tk)
    tok = lambda width: pl.BlockSpec((1, nsub * tq, width), lambda b, i: (b, i, 0))
    in_specs = [tok(MIX_WIDTH), pl.BlockSpec((1, S, KV_WIDTH), lambda b, i: (b, 0, 0)),
                pl.BlockSpec((1, KV_WIDTH, S), lambda b, i: (b, 0, 0)), bias_spec]
    args = [q, k, vt, bias]
    if sel is not None:
        in_specs.append(tok(4 * LANES))
        args.append(sel)
    if gate is not None:
        in_specs.append(tok(2 * LANES))
        args.append(gate)
    if sinks is not None:
        in_specs.append(pl.BlockSpec(memory_space=pltpu.SMEM))
        args.append(sinks)
    return pl.pallas_call(
        functools.partial(_flash_gqa_body, cfg),
        grid=(B, S // (nsub * tq)),
        in_specs=in_specs,
        out_specs=tok(MIX_WIDTH),
        out_shape=jax.ShapeDtypeStruct((B, S, MIX_WIDTH), BF16),
        compiler_params=_cparams(("parallel", "arbitrary"), _ATTN_FLAGS),
        name=f"attn_{mode}{'_sel' if sel is not None else ''}{window or ''}",
    )(*args)


def _flash_mla_body(cfg, q_ref, k_ref, vt_ref, kr_ref, bias_ref, o_ref):
    tq, nhg, tk = cfg
    step_i = pl.program_id(2)
    q0 = step_i * tq
    lane = _lane_iota()
    lo = lane < 64
    slot = (lane >> 4) & 3
    n_pair = 2 * nhg
    lhs = []
    for hg in range(nhg):
        qr = q_ref[0, :, hg * 384 + 256:hg * 384 + 384]
        for pair in range(2):
            qn = q_ref[0, :, hg * 384 + pair * LANES:hg * 384 + (pair + 1) * LANES]
            a = jnp.concatenate([jnp.where(lo, qn, 0), jnp.where(slot == 2 * pair, qr, 0)], 1)
            b = jnp.concatenate([jnp.where(lo, 0, qn), jnp.where(slot == 2 * pair + 1, qr, 0)], 1)
            lhs.append(jnp.concatenate([a, b], 0))
    rows = 2 * tq
    init = (jnp.full((1, rows), NEG_INF, F32), None, None)

    def step(start, states, bias):
        start = pl.multiple_of(start, BAND_BLOCK)
        kr = kr_ref[0, pl.ds(start, tk), :]
        new = []
        for c in range(n_pair):
            kt = jnp.concatenate([k_ref[0, pl.ds(start, tk), c * LANES:(c + 1) * LANES], kr], 1)
            vt = vt_ref[0, c * LANES:(c + 1) * LANES, pl.ds(start, tk)]
            new.append(_online_step(lhs[c], kt, vt, bias, states[c]))
        return tuple(new)

    n_full = q0 // tk
    states = step(n_full * tk, (init,) * n_pair, _tile_lanes(bias_ref[step_i % (tk // tq)], 2))
    states = lax.fori_loop(0, n_full, lambda j, st: step(j * tk, st, None), states)
    for c in range(n_pair):
        o_ref[0, :, c * LANES:(c + 1) * LANES] = _finish(states[c], None).T.astype(o_ref.dtype)


def _flash_mla(q, k, vt, kr, tq=512, tk=512, nhg=2):
    B, S, _ = q.shape
    bias = _diag_bias(tq, tk)
    return pl.pallas_call(
        functools.partial(_flash_mla_body, (tq, nhg, tk)),
        grid=(B, 4 // nhg, S // tq),
        in_specs=[pl.BlockSpec((1, tq, nhg * 384), lambda b, g, i: (b, i, g)),
                  pl.BlockSpec((1, S, nhg * 256), lambda b, g, i: (b, 0, g)),
                  pl.BlockSpec((1, nhg * 256, S), lambda b, g, i: (b, g, 0)),
                  pl.BlockSpec((1, S, LANES), lambda b, g, i: (b, 0, 0)),
                  pl.BlockSpec(bias.shape, lambda b, g, i: (0, 0, 0))],
        out_specs=pl.BlockSpec((1, tq, nhg * 256), lambda b, g, i: (b, i, g)),
        out_shape=jax.ShapeDtypeStruct((B, S, MIX_WIDTH), BF16),
        compiler_params=_cparams(("parallel", "parallel", "arbitrary"), _ATTN_FLAGS),
        name="attn_mla",
    )(q, k, vt, kr, bias)


def _compress_body(kraw_ref, vraw_ref, posa_ref, posb_ref, wk1a_ref, wk1b_ref, wk2_ref,
                   wv1a_ref, wv1b_ref, wv2_ref, cos_ref, sin_ref, kc_ref, vct_ref):
    n = kraw_ref.shape[2]

    def phi(raw, w1a, w1b, w2):
        a = jnp.dot((raw + posa_ref[...]).astype(BF16), w1a[...], preferred_element_type=F32)
        b = jnp.dot((raw + posb_ref[...]).astype(BF16), w1b[...], preferred_element_type=F32)
        h = a + pltpu.roll(b, n - 1, 0)
        return jnp.dot(jax.nn.silu(h).astype(BF16), w2[...], preferred_element_type=F32)

    kc = phi(kraw_ref[0, 0], wk1a_ref, wk1b_ref, wk2_ref)
    kc_ref[0, 0] = _rope_cols(kc, cos_ref[0], sin_ref[0]).astype(BF16)
    vct_ref[0, 0] = phi(vraw_ref[0, 0], wv1a_ref, wv1b_ref, wv2_ref).T.astype(BF16)


def _compress(kraw, vraw, posa, posb, wk, wv, cos_c, sin_c):
    B = kraw.shape[0]
    n = kraw.shape[2]
    raw = pl.BlockSpec((1, 1, n, kraw.shape[3]), lambda b, g: (b, g, 0, 0))
    full = lambda a: pl.BlockSpec(a.shape, lambda b, g: (0,) * a.ndim)
    tab = pl.BlockSpec((1, n, LANES), lambda b, g: (b, 0, 0))
    return pl.pallas_call(
        _compress_body,
        grid=(B, 2),
        in_specs=[raw, raw, full(posa), full(posb)] + [full(w) for w in wk] + [full(w) for w in wv] + [tab, tab],
        out_specs=[pl.BlockSpec((1, 1, n, LANES), lambda b, g: (b, g, 0, 0)),
                   pl.BlockSpec((1, 1, LANES, n), lambda b, g: (b, g, 0, 0))],
        out_shape=[jax.ShapeDtypeStruct((B, 2, n, LANES), BF16), jax.ShapeDtypeStruct((B, 2, LANES, n), BF16)],
        compiler_params=_cparams(("parallel", "parallel")),
        name="nsa_compress",
    )(kraw, vraw, posa, posb, *wk, *wv, cos_c, sin_c)


def _split3(p):
    hi = p.astype(BF16)
    r1 = p - hi.astype(F32)
    mid = r1.astype(BF16)
    lo = (r1 - mid.astype(F32)).astype(BF16)
    return hi, mid, lo


def _cmp_select_body(cfg, q_ref, kc_ref, vct_ref, ovt_ref, gate_ref, o_ref, sb_ref):
    tq, n_cmp_pad, n_blk = cfg
    q0 = pl.program_id(1) * tq
    rows = 8 * tq
    half_rows = rows // 2
    tpos = q0 + (lax.broadcasted_iota(jnp.int32, (1, rows), 1) & (tq - 1))
    blk_end = lax.broadcasted_iota(jnp.int32, (n_cmp_pad, 1), 0) * CMP_STRIDE + (CMP_BLOCK - 1)
    cbias = _tile_lanes(jnp.where(blk_end <= tpos[:, :tq], 0.0, NEG_INF), 8)
    sees_any = tpos >= CMP_BLOCK - 1
    top = _first_half_rows()
    jrow = lax.broadcasted_iota(jnp.int32, (n_blk, 1), 0)
    sub8 = lax.broadcasted_iota(jnp.int32, (8, 1), 0)
    cur = (q0 + lax.broadcasted_iota(jnp.int32, (1, tq), 1)) >> 6
    forced = (jrow == 0) | (jrow == cur) | (jrow == cur - 1)
    ovt = ovt_ref[...]

    for gp in range(2):
        lhs = _pair_lhs([q_ref[0, :, (4 * gp + r) * LANES:(4 * gp + r + 1) * LANES] for r in range(4)])
        s = lax.dot_general(kc_ref[0, gp], lhs, _NT, preferred_element_type=F32)
        s = s + cbias
        m = jnp.max(s, 0, keepdims=True)
        p = jnp.exp2(s - m)
        l = jnp.sum(p, 0, keepdims=True)
        pc = p * jnp.where(sees_any, 1.0 / l, 0.0)

        vct = vct_ref[0, gp]
        pb = pc.astype(BF16)
        pv = (jnp.dot(jnp.where(top, vct, 0), pb[:, :half_rows], preferred_element_type=F32)
              + jnp.dot(jnp.where(top, 0, vct), pb[:, half_rows:], preferred_element_type=F32))
        gate_t = gate_ref[0, :, gp * LANES:(gp + 1) * LANES].T
        ga = _gate_rows(gate_t, [2 * r for r in range(4)])
        gb = _gate_rows(gate_t, [2 * r + 1 for r in range(4)])
        out_t = pv * jnp.where(top, ga, gb)
        for r in range(4):
            o_ref[0, :, (4 * gp + r) * LANES:(4 * gp + r + 1) * LANES] = (
                out_t[:, r * tq:(r + 1) * tq].T.astype(o_ref.dtype))

        for half in range(2):
            c0 = 4 * half * tq
            psum = pc[:, c0:c0 + tq] + pc[:, c0 + tq:c0 + 2 * tq] + pc[:, c0 + 2 * tq:c0 + 3 * tq] + pc[:, c0 + 3 * tq:c0 + 4 * tq]
            imp = sum(jnp.dot(ovt, part, preferred_element_type=F32) for part in _split3(psum))[:n_blk]
            score = jnp.where(forced, SEL_FORCE, jnp.where(jrow <= cur, imp, -SEL_FORCE))
            groups = [score[g:g + 8] for g in range(0, n_blk, 8)]
            ranks = [jnp.zeros((8, tq), F32) for _ in groups]
            for j in range(n_blk):
                row = score[j:j + 1, :]
                for gi, grp in enumerate(groups):
                    if j < 8 * gi:
                        beats = row >= grp
                    elif j >= 8 * gi + 8:
                        beats = row > grp
                    else:
                        beats = (row > grp) | ((row == grp) & (sub8 > j - 8 * gi))
                    ranks[gi] = ranks[gi] + jnp.where(beats, 1.0, 0.0)
            rank = jnp.concatenate(ranks, 0)
            bias = jnp.where(rank < min(N_SEL, n_blk), 0.0, NEG_INF)
            bias = jnp.concatenate([bias, jnp.zeros((LANES - n_blk, tq), F32)], 0)
            sb_ref[0, :, (2 * gp + half) * LANES:(2 * gp + half + 1) * LANES] = bias.T.astype(BF16)


def _cmp_select(q, kc, vct, ovt, gate, tq=128):
    B, S, _ = q.shape
    n_cmp_pad = kc.shape[2]
    n_blk = S // SEL_BLOCK
    tok = lambda width: pl.BlockSpec((1, tq, width), lambda b, i: (b, i, 0))
    return pl.pallas_call(
        functools.partial(_cmp_select_body, (tq, n_cmp_pad, n_blk)),
        grid=(B, S // tq),
        in_specs=[tok(MIX_WIDTH), pl.BlockSpec((1, 2, n_cmp_pad, LANES), lambda b, i: (b, 0, 0, 0)),
                  pl.BlockSpec((1, 2, LANES, n_cmp_pad), lambda b, i: (b, 0, 0, 0)),
                  pl.BlockSpec(ovt.shape, lambda b, i: (0, 0)), tok(2 * LANES)],
        out_specs=[tok(MIX_WIDTH), tok(4 * LANES)],
        out_shape=[jax.ShapeDtypeStruct((B, S, MIX_WIDTH), BF16), jax.ShapeDtypeStruct((B, S, 4 * LANES), BF16)],
        compiler_params=_cparams(("parallel", "parallel")),
        name="nsa_cmp_select",
    )(q, kc, vct, ovt, gate)


def _out_body(n_o, *refs):
    o_refs = refs[:n_o]
    z_ref, x_ref, p_ref, wo_ref, g_ref, b_ref, wg_ref, wp_ref, out_ref = refs[n_o:]
    o = o_refs[0][0].astype(F32)
    for r in o_refs[1:]:
        o = o + r[0].astype(F32)
    gated = (o * jax.nn.silu(z_ref[0].astype(F32))).astype(BF16)
    y = jnp.dot(gated, wo_ref[...], preferred_element_type=F32)
    u = DEEPNORM_ALPHA * x_ref[0] + y
    mu = jnp.mean(u, -1, keepdims=True)
    var = jnp.mean(jnp.square(u - mu), -1, keepdims=True)
    xn = (u - mu) * lax.rsqrt(var + LN_EPS) * g_ref[...] + b_ref[...]
    gate = jax.nn.sigmoid(jnp.dot(xn.astype(BF16), wg_ref[...], preferred_element_type=F32))
    emb = jnp.dot(p_ref[0, 0].astype(BF16), wp_ref[...], preferred_element_type=F32)
    out_ref[0] = xn + gate * emb


def _out_block(o_list, z, x, p, layer, wo, g, b, wg, wp, tm=512):
    B, S, _ = x.shape
    tok = lambda width: pl.BlockSpec((1, tm, width), lambda bb, i: (bb, i, 0))
    full = lambda a: pl.BlockSpec(a.shape, lambda bb, i: (0,) * a.ndim)
    n_o = len(o_list)
    return pl.pallas_call(
        functools.partial(_out_body, n_o),
        grid=(B, S // tm),
        in_specs=[tok(MIX_WIDTH)] * n_o + [tok(MIX_WIDTH), tok(D_MODEL),
                                           pl.BlockSpec((1, 1, tm, PLE_DIM), lambda bb, i: (layer, bb, i, 0)),
                                           full(wo), full(g), full(b), full(wg), full(wp)],
        out_specs=tok(D_MODEL),
        out_shape=jax.ShapeDtypeStruct((B, S, D_MODEL), F32),
        compiler_params=_cparams(("parallel", "parallel")),
        name="out_block",
    )(*o_list, z, x, p, wo, g, b, wg, wp)


def _swa_weights(w_in):
    q, k, v, z = jnp.split(w_in, [MIX_WIDTH, MIX_WIDTH + KV_WIDTH, MIX_WIDTH + 2 * KV_WIDTH], axis=1)
    return jnp.concatenate([q[:, _QPERM], z[:, _OPERM], k[:, _KPERM], v], 1).astype(BF16)


def _nsa_weights(w_in):
    cuts = np.cumsum([MIX_WIDTH] + [KV_WIDTH] * 6 + [3 * N_HEADS]).tolist()
    q, k_c, v_c, k_s, v_s, k_w, v_w, gl, z = jnp.split(w_in, cuts, axis=1)
    q = q[:, _QPERM]
    gl = jnp.concatenate(
        [jnp.pad(gl[:, np.array([br * N_HEADS + _POS_HEAD[gp * 8 + c] for br in range(3) for c in range(8)])],
                 ((0, 0), (0, LANES - 24))) for gp in range(2)], 1)
    return jnp.concatenate([q, z[:, _OPERM], k_s[:, _KPERM], v_s, k_w[:, _KPERM], v_w, k_c, v_c, gl], 1).astype(BF16)


def _phi_weights(w1, w2, rope_layout):
    h = w1.shape[1]
    w1 = w1.reshape(2, CMP_STRIDE, HEAD_DIM, h)
    eye = jnp.eye(2, dtype=w1.dtype)
    w1a, w1b = (jnp.einsum("ldj,gh->lgdhj", w1[t], eye).reshape(CMP_STRIDE * 2 * HEAD_DIM, 2 * h) for t in range(2))
    if rope_layout:
        w2 = jnp.einsum("jpi,gh->gjphi", w2.reshape(h, 2, 32), eye).reshape(2 * h, LANES)
    else:
        w2 = jnp.einsum("jd,gh->gjhd", w2, eye).reshape(2 * h, LANES)
    return w1a.astype(BF16), w1b.astype(BF16), w2.astype(BF16)


def _overlap_matrix_t(n_cmp, n_cmp_pad, n_blk):
    cstart = np.arange(n_cmp_pad) * CMP_STRIDE
    j = np.arange(LANES)
    ov = (cstart[None, :] < (j[:, None] + 1) * SEL_BLOCK) & (cstart[None, :] + CMP_BLOCK > j[:, None] * SEL_BLOCK)
    ov &= (np.arange(n_cmp_pad)[None, :] < n_cmp) & (j[:, None] < n_blk)
    return jnp.asarray(ov.astype(np.float32), dtype=BF16)


def _layer_swa(x, tabs64, w_in, sinks):
    q, z, k, vt = _proj_gqa(x, _swa_weights(w_in), *tabs64, n_kv=1, with_cmp=False)
    o = _flash_gqa(q, k, vt, mode="band", window=SWA_WINDOW, sinks=sinks.astype(F32), nsub=2)
    return [o], z, _OPERM


def _layer_mla(x, tabs32, w_in, q_norm, kv_norm, w_uq, w_ukv):
    lat = MLA_Q_LORA + MLA_KV_LORA
    w1 = jnp.concatenate([w_in[:, :lat], w_in[:, lat:lat + MLA_ROPE][:, _KRPERM], w_in[:, lat + MLA_ROPE:]], 1)
    wkv = jnp.concatenate([w_ukv[:, _KNPERM], w_ukv[:, _VPERM]], 1)
    q, k, vt, kr, z = _proj_mla(x, w1.astype(BF16), q_norm[None, :], kv_norm[None, :],
                                w_uq[:, _UQPERM].astype(BF16), wkv.astype(BF16), *tabs32)
    return [_flash_mla(q, k, vt, kr)], z, None


def _layer_nsa(x, tabs64, positions, w_in, cmp_pos, phi_k1, phi_k2, phi_v1, phi_v2):
    B, S, _ = x.shape
    q, z, ks, vst, kw, vwt, kraw, vraw, gate = _proj_gqa(x, _nsa_weights(w_in), *tabs64, n_kv=2, with_cmp=True)
    n_chunk = S // CMP_STRIDE
    n_cmp = (S - CMP_BLOCK) // CMP_STRIDE + 1
    kraw = kraw.reshape(B, 2, n_chunk, CMP_STRIDE * LANES)
    vraw = vraw.reshape(B, 2, n_chunk, CMP_STRIDE * LANES)
    pos2 = jnp.tile(cmp_pos.reshape(2, CMP_STRIDE, 1, HEAD_DIM), (1, 1, 2, 1)).reshape(2, 1, CMP_STRIDE * LANES)
    end = jnp.minimum(jnp.arange(n_chunk) * CMP_STRIDE + CMP_BLOCK - 1, S - 1)
    cos_c, sin_c = _rope_tables(positions[:, end].reshape(-1), HEAD_DIM // 2)
    kc, vct = _compress(kraw, vraw, pos2[0], pos2[1], _phi_weights(phi_k1, phi_k2, True),
                        _phi_weights(phi_v1, phi_v2, False),
                        cos_c.reshape(B, n_chunk, LANES), sin_c.reshape(B, n_chunk, LANES))
    ovt = _overlap_matrix_t(n_cmp, n_chunk, S // SEL_BLOCK)
    o_cmp, sel = _cmp_select(q, kc, vct, ovt, gate)
    o_slc = _flash_gqa(q, ks, vst, mode="causal", sel=sel, gate=gate, gate_base=8, tq=256)
    o_win = _flash_gqa(q, kw, vwt, mode="band", window=NSA_WINDOW, gate=gate, gate_base=16, nsub=2)
    return [o_cmp, o_slc, o_win], z, _OPERM


def kernel(x, p, positions, w_out, ln_g, ln_b, pe_gate, pe_proj, l0_w_in, l0_sinks, l1_w_in, l1_q_norm, l1_kv_norm, l1_w_uq, l1_w_ukv, l2_w_in, l2_cmp_pos, l2_phi_k1, l2_phi_k2, l2_phi_v1, l2_phi_v2, l3_w_in, l3_sinks):
    B, S, _ = x.shape
    pos_flat = positions.reshape(-1)
    tabs64 = [t.reshape(B, S, LANES) for t in _rope_tables(pos_flat, HEAD_DIM // 2)]
    tabs32 = [t.reshape(B, S, LANES) for t in _rope_tables(pos_flat, MLA_ROPE // 2)]
    layers = (
        lambda h: _layer_swa(h, tabs64, l0_w_in, l0_sinks),
        lambda h: _layer_mla(h, tabs32, l1_w_in, l1_q_norm, l1_kv_norm, l1_w_uq, l1_w_ukv),
        lambda h: _layer_nsa(h, tabs64, positions, l2_w_in, l2_cmp_pos, l2_phi_k1, l2_phi_k2, l2_phi_v1, l2_phi_v2),
        lambda h: _layer_swa(h, tabs64, l3_w_in, l3_sinks),
    )
    for i, layer in enumerate(layers):
        o_list, z, operm = layer(x)
        wo = w_out[i] if operm is None else w_out[i][operm, :]
        x = _out_block(o_list, z, x, p, i, wo.astype(BF16), ln_g[i][None, :], ln_b[i][None, :],
                       pe_gate[i].astype(BF16), pe_proj[i].astype(BF16))
    return x
```

```python
import functools

import numpy as np
import jax
import jax.numpy as jnp
from jax import lax
from jax.experimental import pallas as pl
from jax.experimental.pallas import tpu as pltpu

F32 = jnp.float32
BF16 = jnp.bfloat16

LANES = 128
VMEM_LIMIT = 56 * 1024 * 1024

D_MODEL = 1024
DEPTH = 4
HEAD_DIM = 64
N_HEADS = 16
MIX_WIDTH = 1024
ROPE_THETA = 10000.0
NEG_INF = -1e30
PLE_DIM = 256
DEEPNORM_ALPHA = (2 * DEPTH) ** 0.25
LN_EPS = 1e-5
RMS_EPS = 1e-6

SWA_WINDOW = 128
KV_GROUPS = 4
KV_WIDTH = KV_GROUPS * HEAD_DIM

MLA_Q_LORA = 384
MLA_KV_LORA = 256
MLA_NOPE = 64
MLA_ROPE = 32
MLA_V = 64

CMP_BLOCK = 32
CMP_STRIDE = 16
SEL_BLOCK = 64
N_SEL = 8
SEL_FORCE = 1e9
NSA_WINDOW = 512
PHI_HIDDEN = 256

BAND_BLOCK = 128
OUT_ROWS = 128
LOG2E = 1.4426950408889634


def _cparams(sem, flags=None):
    return pltpu.CompilerParams(dimension_semantics=sem, vmem_limit_bytes=VMEM_LIMIT, flags=flags)


_ATTN_FLAGS = None


def _pair_head(gp, r, half):
    return (2 * gp + half) * 4 + r


def _gqa_perms():
    qperm, kperm, operm = [], [], []
    for gp in range(2):
        for r in range(4):
            for part in range(2):
                for half in range(2):
                    h = _pair_head(gp, r, half)
                    qperm += [h * 64 + part * 32 + i for i in range(32)]
            for half in range(2):
                h = _pair_head(gp, r, half)
                operm += [h * 64 + d for d in range(64)]
        for part in range(2):
            for half in range(2):
                g = 2 * gp + half
                kperm += [g * 64 + part * 32 + i for i in range(32)]
    return np.array(qperm), np.array(kperm), np.array(operm)


_QPERM, _KPERM, _OPERM = _gqa_perms()
_POS_HEAD = np.array([_pair_head(gp, r, half) for gp in range(2) for r in range(4) for half in range(2)])


def _mla_perms():
    uq, kr = [], []
    for hg in range(4):
        for a in range(4):
            uq += [(4 * hg + a) * 96 + d for d in range(64)]
        for part in range(2):
            for a in range(4):
                uq += [(4 * hg + a) * 96 + 64 + part * 16 + i for i in range(16)]
    for part in range(2):
        for _ in range(4):
            kr += [part * 16 + i for i in range(16)]
    kn = [h * 128 + d for h in range(16) for d in range(64)]
    vv = [h * 128 + 64 + d for h in range(16) for d in range(64)]
    return np.array(uq), np.array(kr), np.array(kn), np.array(vv)


_UQPERM, _KRPERM, _KNPERM, _VPERM = _mla_perms()


def _rope_lane_tables(half):
    inv = ROPE_THETA ** (-jnp.arange(half, dtype=F32) / half)
    lane = np.arange(LANES)
    inv_l = inv[lane % half][None, :]
    sign = np.where(lane < 64, -1.0, 1.0).astype(np.float32)[None, :]
    return inv_l, jnp.asarray(sign)


def _rope_table_body(pos_ref, inv_ref, sign_ref, cos_ref, sin_ref):
    ang = pos_ref[...].astype(F32) * inv_ref[...]
    cos_ref[...] = jnp.cos(ang)
    sin_ref[...] = jnp.sin(ang) * sign_ref[...]


def _rope_tables(pos, half):
    n = pos.shape[0]
    tr = min(n, 1024)
    inv_l, sign = _rope_lane_tables(half)
    row = pl.BlockSpec((tr, LANES), lambda i: (i, 0))
    cst = pl.BlockSpec((1, LANES), lambda i: (0, 0))
    return pl.pallas_call(
        _rope_table_body,
        grid=(n // tr,),
        in_specs=[pl.BlockSpec((tr, 1), lambda i: (i, 0)), cst, cst],
        out_specs=[row, row],
        out_shape=[jax.ShapeDtypeStruct((n, LANES), F32)] * 2,
        compiler_params=_cparams(("parallel",)),
        name="rope_tables",
    )(pos.reshape(n, 1), inv_l, sign)


def _rope_cols(v, cos, sin):
    return v * cos + pltpu.roll(v, 64, 1) * sin


def _proj_gqa_body(n_kv, with_cmp, x_ref, w_ref, cos_ref, sin_ref, *outs):
    xb = x_ref[0].astype(BF16)
    cos = cos_ref[0]
    sin = sin_ref[0]

    def mm(c0, width):
        return jnp.dot(xb, w_ref[:, c0:c0 + width], preferred_element_type=F32)

    q_ref, z_ref = outs[0], outs[1]
    for c in range(0, MIX_WIDTH, 512):
        acc = mm(c, 512) * (HEAD_DIM ** -0.5 * LOG2E)
        for j in range(0, 512, LANES):
            q_ref[0, :, c + j:c + j + LANES] = _rope_cols(acc[:, j:j + LANES], cos, sin).astype(BF16)
    for c in range(0, MIX_WIDTH, 512):
        z_ref[0, :, c:c + 512] = mm(MIX_WIDTH + c, 512).astype(BF16)
    col = 2 * MIX_WIDTH
    o = 2
    for _ in range(n_kv):
        acc = mm(col, 2 * KV_WIDTH)
        for j in range(0, KV_WIDTH, LANES):
            outs[o][0, :, j:j + LANES] = _rope_cols(acc[:, j:j + LANES], cos, sin).astype(BF16)
        outs[o + 1][0] = acc[:, KV_WIDTH:].T.astype(BF16)
        col += 2 * KV_WIDTH
        o += 2
    if with_cmp:
        acc = mm(col, 2 * KV_WIDTH)
        for t in range(2):
            for gp in range(2):
                c0 = t * KV_WIDTH + gp * LANES
                outs[o + t][0, gp] = acc[:, c0:c0 + LANES]
        col += 2 * KV_WIDTH
        outs[o + 2][0] = jax.nn.sigmoid(mm(col, 2 * LANES))


def _proj_gqa(x, w, cos, sin, n_kv, with_cmp, tm=512):
    B, S, _ = x.shape
    n = w.shape[1]
    tok = lambda width: pl.BlockSpec((1, tm, width), lambda b, i: (b, i, 0))
    out_specs = [tok(MIX_WIDTH), tok(MIX_WIDTH)]
    out_shape = [jax.ShapeDtypeStruct((B, S, MIX_WIDTH), BF16)] * 2
    for _ in range(n_kv):
        out_specs += [tok(KV_WIDTH), pl.BlockSpec((1, KV_WIDTH, tm), lambda b, i: (b, 0, i))]
        out_shape += [jax.ShapeDtypeStruct((B, S, KV_WIDTH), BF16), jax.ShapeDtypeStruct((B, KV_WIDTH, S), BF16)]
    if with_cmp:
        raw = pl.BlockSpec((1, 2, tm, LANES), lambda b, i: (b, 0, i, 0))
        out_specs += [raw, raw, tok(2 * LANES)]
        out_shape += [jax.ShapeDtypeStruct((B, 2, S, LANES), F32)] * 2
        out_shape += [jax.ShapeDtypeStruct((B, S, 2 * LANES), F32)]
    return pl.pallas_call(
        functools.partial(_proj_gqa_body, n_kv, with_cmp),
        grid=(B, S // tm),
        in_specs=[tok(D_MODEL), pl.BlockSpec((D_MODEL, n), lambda b, i: (0, 0)), tok(LANES), tok(LANES)],
        out_specs=out_specs,
        out_shape=out_shape,
        compiler_params=_cparams(("parallel", "parallel")),
        name="proj_gqa",
    )(x, w, cos, sin)


def _rms(v, g):
    return v * lax.rsqrt(jnp.mean(v * v, -1, keepdims=True) + RMS_EPS) * g


def _proj_mla_body(x_ref, w1_ref, qn_ref, kvn_ref, wuq_ref, wkv_ref, cos_ref, sin_ref,
                   q_ref, k_ref, v_ref, kr_ref, z_ref):
    xb = x_ref[0].astype(BF16)
    cos = cos_ref[0]
    sin = sin_ref[0]
    lat = MLA_Q_LORA + MLA_KV_LORA
    c = jnp.dot(xb, w1_ref[:, :lat + LANES], preferred_element_type=F32)
    cq = _rms(c[:, :MLA_Q_LORA], qn_ref[...]).astype(BF16)
    ckv = _rms(c[:, MLA_Q_LORA:lat], kvn_ref[...]).astype(BF16)
    kr_ref[0] = _rope_cols(c[:, lat:lat + LANES], cos, sin).astype(BF16)
    scale = (MLA_NOPE + MLA_ROPE) ** -0.5 * LOG2E
    for hg in range(4):
        c0 = hg * 384
        q = jnp.dot(cq, wuq_ref[:, c0:c0 + 384], preferred_element_type=F32) * scale
        q_ref[0, :, c0:c0 + 256] = q[:, :256].astype(BF16)
        q_ref[0, :, c0 + 256:c0 + 384] = _rope_cols(q[:, 256:], cos, sin).astype(BF16)
    for c0 in range(0, MIX_WIDTH, 512):
        k_ref[0, :, c0:c0 + 512] = jnp.dot(ckv, wkv_ref[:, c0:c0 + 512], preferred_element_type=F32).astype(BF16)
        v_ref[0, c0:c0 + 512, :] = jnp.dot(ckv, wkv_ref[:, MIX_WIDTH + c0:MIX_WIDTH + c0 + 512],
                                            preferred_element_type=F32).T.astype(BF16)
        z_ref[0, :, c0:c0 + 512] = jnp.dot(xb, w1_ref[:, lat + LANES + c0:lat + LANES + c0 + 512],
                                            preferred_element_type=F32).astype(BF16)


def _proj_mla(x, w1, qn, kvn, wuq, wkv, cos, sin, tm=512):
    B, S, _ = x.shape
    tok = lambda width: pl.BlockSpec((1, tm, width), lambda b, i: (b, i, 0))
    full = lambda a: pl.BlockSpec(a.shape, lambda b, i: (0,) * a.ndim)
    widths = [16 * 96, MIX_WIDTH, MIX_WIDTH, LANES, MIX_WIDTH]
    out_specs = [tok(w) for w in widths]
    out_shape = [jax.ShapeDtypeStruct((B, S, w), BF16) for w in widths]
    out_specs[2] = pl.BlockSpec((1, MIX_WIDTH, tm), lambda b, i: (b, 0, i))
    out_shape[2] = jax.ShapeDtypeStruct((B, MIX_WIDTH, S), BF16)
    return pl.pallas_call(
        _proj_mla_body,
        grid=(B, S // tm),
        in_specs=[tok(D_MODEL), full(w1), full(qn), full(kvn), full(wuq), full(wkv), tok(LANES), tok(LANES)],
        out_specs=out_specs,
        out_shape=out_shape,
        compiler_params=_cparams(("parallel", "parallel")),
        name="proj_mla",
    )(x, w1, qn, kvn, wuq, wkv, cos, sin)


_NT = (((1,), (1,)), ((), ()))


def _lane_iota():
    return lax.broadcasted_iota(jnp.int32, (1, LANES), 1)


def _first_half_rows():
    return lax.broadcasted_iota(jnp.int32, (LANES, 1), 0) < 64


def _pair_lhs(q_cols, extra_a=None, extra_b=None):
    m_a = ((_lane_iota() >> 5) & 1) == 0
    a_rows, b_rows = [], []
    for qc in q_cols:
        a = jnp.where(m_a, qc, 0)
        b = jnp.where(m_a, 0, qc)
        if extra_a is not None:
            a = jnp.concatenate([a, extra_a], 1)
            b = jnp.concatenate([b, extra_b], 1)
        a_rows.append(a)
        b_rows.append(b)
    return jnp.concatenate(a_rows + b_rows, 0)


def _tile_lanes(bias, reps):
    return jnp.concatenate([bias] * reps, 1)


def _online_step(lhs, kt, vt, bias, state):
    return _softmax_pv(_scores(lhs, kt), vt, bias, state)


def _scores(lhs, kt):
    return lax.dot_general(kt, lhs, _NT, preferred_element_type=F32)


def _softmax_pv(s, vt, bias, state):
    m, acc_a, acc_b = state
    half = s.shape[1] // 2
    if bias is not None:
        s = s + bias
    m_new = jnp.maximum(m, jnp.max(s, 0, keepdims=True))
    p = jnp.exp2(s - m_new).astype(BF16)
    top = _first_half_rows()
    pv_a = jnp.dot(jnp.where(top, vt, 1), p[:, :half], preferred_element_type=F32)
    pv_b = jnp.dot(jnp.where(top, 1, vt), p[:, half:], preferred_element_type=F32)
    if acc_a is None:
        return m_new, pv_a, pv_b
    alpha = jnp.exp2(m - m_new)
    return m_new, acc_a * alpha[:, :half] + pv_a, acc_b * alpha[:, half:] + pv_b


def _causal_sweep(score_fn, pv_fn, n_chain, n_full, tk, bias, init, buf_a, buf_b):
    def scores_to(buf, start, bias=None):
        start = pl.multiple_of(start, BAND_BLOCK)
        for c in range(n_chain):
            s = score_fn(c, start)
            buf[c] = s if bias is None else s + bias

    def consume(buf, start, states):
        start = pl.multiple_of(start, BAND_BLOCK)
        return tuple(pv_fn(c, buf[c], start, states[c]) for c in range(n_chain))

    diag_start = n_full * tk
    scores_to(buf_a, diag_start, bias)

    def two_tiles(i, carry):
        states, prev = carry
        scores_to(buf_b, 2 * i * tk)
        states = consume(buf_a, prev, states)
        scores_to(buf_a, (2 * i + 1) * tk)
        states = consume(buf_b, 2 * i * tk, states)
        return states, (2 * i + 1) * tk

    states, prev = lax.fori_loop(0, n_full // 2, two_tiles, (init, diag_start))

    def odd_tail(states):
        last = (n_full - 1) * tk
        scores_to(buf_b, last)
        return consume(buf_b, last, consume(buf_a, prev, states))

    return lax.cond(n_full % 2 == 1, odd_tail, lambda st: consume(buf_a, prev, st), states)


def _finish(state, gate_rows):
    _, acc_a, acc_b = state
    half = acc_a.shape[1]
    inv_a = 1.0 / acc_a[64:65, :]
    inv_b = 1.0 / acc_b[0:1, :]
    if gate_rows is not None:
        inv_a = inv_a * gate_rows[:, :half]
        inv_b = inv_b * gate_rows[:, half:]
    return jnp.where(_first_half_rows(), acc_a * inv_a, acc_b * inv_b)


def _gate_rows(gate_t, cols):
    return jnp.concatenate([gate_t[c:c + 1, :] for c in cols], 1)


def _flash_gqa_body(cfg, *refs):
    tq, nsub, n_prev, mode, use_sel, use_sink, gate_base, tk = cfg
    refs = list(refs)
    q_ref, k_ref, vt_ref, bias_ref = refs[:4]
    nxt = 4
    sb_ref = gate_ref = sink_ref = None
    if use_sel:
        sb_ref = refs[nxt]; nxt += 1
    if gate_base is not None:
        gate_ref = refs[nxt]; nxt += 1
    if use_sink:
        sink_ref = refs[nxt]; nxt += 1
    o_ref = refs[nxt]

    step_i = pl.program_id(1)
    rows = 8 * tq
    top = _first_half_rows()
    chains = [(sub, gp) for sub in range(nsub) for gp in range(2)]
    lhs, init = [], []
    for sub, gp in chains:
        tok = slice(sub * tq, (sub + 1) * tq)
        q_cols = [q_ref[0, tok, (4 * gp + r) * LANES:(4 * gp + r + 1) * LANES] for r in range(4)]
        if use_sel:
            lhs.append(_pair_lhs(q_cols, sb_ref[0, tok, 2 * gp * LANES:(2 * gp + 1) * LANES],
                                 sb_ref[0, tok, (2 * gp + 1) * LANES:(2 * gp + 2) * LANES]))
        else:
            lhs.append(_pair_lhs(q_cols))
        if use_sink:
            m0 = jnp.concatenate([jnp.full((1, tq), sink_ref[_pair_head(gp, r, half)] * LOG2E, F32)
                                  for half in range(2) for r in range(4)], 1)
            init.append((m0, jnp.where(top, jnp.zeros((LANES, rows // 2), F32), 1.0),
                         jnp.where(top, jnp.ones((LANES, rows // 2), F32), 0.0)))
        else:
            init.append((jnp.full((1, rows), NEG_INF, F32), None, None))

    def step(starts, states, biases):
        starts = [pl.multiple_of(st, BAND_BLOCK) for st in starts]
        s = []
        for c, (sub, gp) in enumerate(chains):
            kt = k_ref[0, pl.ds(starts[c], tk), gp * LANES:(gp + 1) * LANES]
            if use_sel:
                blk = (starts[c] + lax.broadcasted_iota(jnp.int32, (tk, LANES), 0)) >> 6
                onehot = (blk == lax.broadcasted_iota(jnp.int32, (tk, LANES), 1)).astype(BF16)
                kt = jnp.concatenate([kt, onehot], 1)
            s.append(_scores(lhs[c], kt))
        return tuple(
            _softmax_pv(s[c], vt_ref[0, gp * LANES:(gp + 1) * LANES, pl.ds(starts[c], tk)], biases[c], states[c])
            for c, (sub, gp) in enumerate(chains))

    if mode == "band":
        tiles = [step_i * nsub + sub for sub, gp in chains]
        states = step([jnp.maximum(t * tq - (tk - tq), 0) for t in tiles], init,
                      [_tile_lanes(bias_ref[jnp.minimum(t, n_prev)], 8) for t in tiles])
    else:
        buf_a, buf_b = refs[nxt + 1], refs[nxt + 2]

        def score_fn(c, start):
            gp = chains[c][1]
            kt = k_ref[0, pl.ds(start, tk), gp * LANES:(gp + 1) * LANES]
            if use_sel:
                blk = (start + lax.broadcasted_iota(jnp.int32, (tk, LANES), 0)) >> 6
                onehot = (blk == lax.broadcasted_iota(jnp.int32, (tk, LANES), 1)).astype(BF16)
                kt = jnp.concatenate([kt, onehot], 1)
            return _scores(lhs[c], kt)

        def pv_fn(c, s, start, state):
            gp = chains[c][1]
            return _softmax_pv(s, vt_ref[0, gp * LANES:(gp + 1) * LANES, pl.ds(start, tk)], None, state)

        zeros = jnp.zeros((LANES, rows // 2), F32)
        states = _causal_sweep(score_fn, pv_fn, len(chains), (step_i * tq) // tk, tk,
                               _tile_lanes(bias_ref[step_i % (tk // tq)], 8),
                               tuple((m0, zeros, zeros) for m0, _, _ in init), buf_a, buf_b)

    for c, (sub, gp) in enumerate(chains):
        tok = slice(sub * tq, (sub + 1) * tq)
        gate_rows = None
        if gate_base is not None:
            gate_t = gate_ref[0, tok, gp * LANES:(gp + 1) * LANES].T
            gate_rows = _gate_rows(gate_t, [gate_base + 2 * r for r in range(4)]
                                   + [gate_base + 2 * r + 1 for r in range(4)])
        out_t = _finish(states[c], gate_rows)
        for r in range(4):
            o_ref[0, tok, (4 * gp + r) * LANES:(4 * gp + r + 1) * LANES] = (
                out_t[:, r * tq:(r + 1) * tq].T.astype(o_ref.dtype))


def _band_bias(window, tq, tk):
    pad = tk - tq
    r = np.arange(tk)[:, None]
    c = np.arange(tq)[None, :]
    out = []
    for v in range(pad // tq + 1):
        d = r - c - (v * tq if v < pad // tq else pad)
        out.append(np.where((d <= 0) & (d > -window), 0.0, NEG_INF))
    return jnp.asarray(np.stack(out), F32)


def _diag_bias(tq, tk):
    r = np.arange(tk)[:, None]
    c = np.arange(tq)[None, :]
    return jnp.asarray(np.stack([np.where(r <= v * tq + c, 0.0, NEG_INF) for v in range(tk // tq)]), F32)


def _flash_gqa(q, k, vt, *, mode, window=None, sel=None, gate=None, gate_base=None, sinks=None,
               tq=128, nsub=1, tk=512):
    B, S, _ = q.shape
    n_prev = 0
    if mode == "band":
        pad = -(-(window - 1) // BAND_BLOCK) * BAND_BLOCK
        tk = pad + tq
        bias = _band_bias(window, tq, tk)
        n_prev = pad // tq
    else:
        assert nsub == 1
        bias = _diag_bias(tq, tk)
    bias_spec = pl.BlockSpec(bias.shape, lambda b, i: (0, 0, 0))
    cfg = (tq, nsub, n_prev, mode, sel is not None, sinks is not None, gate_base if gate is not None else None, tk)
    tok = lambda width: pl.BlockSpec((1, nsub * tq, width), lambda b, i: (b, i, 0))
    in_specs = [tok(MIX_WIDTH), pl.BlockSpec((1, S, KV_WIDTH), lambda b, i: (b, 0, 0)),
                pl.BlockSpec((1, KV_WIDTH, S), lambda b, i: (b, 0, 0)), bias_spec]
    args = [q, k, vt, bias]
    if sel is not None:
        in_specs.append(tok(4 * LANES))
        args.append(sel)
    if gate is not None:
        in_specs.append(tok(2 * LANES))
        args.append(gate)
    if sinks is not None:
        in_specs.append(pl.BlockSpec(memory_space=pltpu.SMEM))
        args.append(sinks)
    return pl.pallas_call(
        functools.partial(_flash_gqa_body, cfg),
        grid=(B, S // (nsub * tq)),
        in_specs=in_specs,
        out_specs=tok(MIX_WIDTH),
        out_shape=jax.ShapeDtypeStruct((B, S, MIX_WIDTH), BF16),
        scratch_shapes=[] if mode == "band" else [pltpu.VMEM((2, tk, 8 * tq), F32)] * 2,
        compiler_params=_cparams(("parallel", "arbitrary"), _ATTN_FLAGS),
        name=f"attn_{mode}{'_sel' if sel is not None else ''}{window or ''}",
    )(*args)


def _flash_mla_body(cfg, q_ref, k_ref, vt_ref, kr_ref, bias_ref, o_ref, buf_a, buf_b):
    tq, nhg, tk = cfg
    step_i = pl.program_id(2)
    q0 = step_i * tq
    lane = _lane_iota()
    lo = lane < 64
    slot = (lane >> 4) & 3
    n_pair = 2 * nhg
    lhs = []
    for hg in range(nhg):
        qr = q_ref[0, :, hg * 384 + 256:hg * 384 + 384]
        for pair in range(2):
            qn = q_ref[0, :, hg * 384 + pair * LANES:hg * 384 + (pair + 1) * LANES]
            a = jnp.concatenate([jnp.where(lo, qn, 0), jnp.where(slot == 2 * pair, qr, 0)], 1)
            b = jnp.concatenate([jnp.where(lo, 0, qn), jnp.where(slot == 2 * pair + 1, qr, 0)], 1)
            lhs.append(jnp.concatenate([a, b], 0))
    rows = 2 * tq
    init = (jnp.full((1, rows), NEG_INF, F32), jnp.zeros((LANES, tq), F32), jnp.zeros((LANES, tq), F32))

    def score_fn(c, start):
        kt = jnp.concatenate([k_ref[0, pl.ds(start, tk), c * LANES:(c + 1) * LANES],
                              kr_ref[0, pl.ds(start, tk), :]], 1)
        return _scores(lhs[c], kt)

    def pv_fn(c, s, start, state):
        return _softmax_pv(s, vt_ref[0, c * LANES:(c + 1) * LANES, pl.ds(start, tk)], None, state)

    states = _causal_sweep(score_fn, pv_fn, n_pair, q0 // tk, tk, _tile_lanes(bias_ref[step_i % (tk // tq)], 2),
                           (init,) * n_pair, buf_a, buf_b)
    for c in range(n_pair):
        o_ref[0, :, c * LANES:(c + 1) * LANES] = _finish(states[c], None).T.astype(o_ref.dtype)


def _flash_mla(q, k, vt, kr, tq=512, tk=512, nhg=2):
    B, S, _ = q.shape
    bias = _diag_bias(tq, tk)
    return pl.pallas_call(
        functools.partial(_flash_mla_body, (tq, nhg, tk)),
        grid=(B, 4 // nhg, S // tq),
        in_specs=[pl.BlockSpec((1, tq, nhg * 384), lambda b, g, i: (b, i, g)),
                  pl.BlockSpec((1, S, nhg * 256), lambda b, g, i: (b, 0, g)),
                  pl.BlockSpec((1, nhg * 256, S), lambda b, g, i: (b, g, 0)),
                  pl.BlockSpec((1, S, LANES), lambda b, g, i: (b, 0, 0)),
                  pl.BlockSpec(bias.shape, lambda b, g, i: (0, 0, 0))],
        out_specs=pl.BlockSpec((1, tq, nhg * 256), lambda b, g, i: (b, i, g)),
        out_shape=jax.ShapeDtypeStruct((B, S, MIX_WIDTH), BF16),
        scratch_shapes=[pltpu.VMEM((2 * nhg, tk, 2 * tq), F32)] * 2,
        compiler_params=_cparams(("parallel", "parallel", "arbitrary"), _ATTN_FLAGS),
        name="attn_mla",
    )(q, k, vt, kr, bias)


def _compress_body(kraw_ref, vraw_ref, posa_ref, posb_ref, wk1a_ref, wk1b_ref, wk2_ref,
                   wv1a_ref, wv1b_ref, wv2_ref, cos_ref, sin_ref, kc_ref, vct_ref):
    n = kraw_ref.shape[2]

    def phi(raw, w1a, w1b, w2):
        a = jnp.dot((raw + posa_ref[...]).astype(BF16), w1a[...], preferred_element_type=F32)
        b = jnp.dot((raw + posb_ref[...]).astype(BF16), w1b[...], preferred_element_type=F32)
        h = a + pltpu.roll(b, n - 1, 0)
        return jnp.dot(jax.nn.silu(h).astype(BF16), w2[...], preferred_element_type=F32)

    kc = phi(kraw_ref[0, 0], wk1a_ref, wk1b_ref, wk2_ref)
    kc_ref[0, 0] = _rope_cols(kc, cos_ref[0], sin_ref[0]).astype(BF16)
    vct_ref[0, 0] = phi(vraw_ref[0, 0], wv1a_ref, wv1b_ref, wv2_ref).T.astype(BF16)


def _compress(kraw, vraw, posa, posb, wk, wv, cos_c, sin_c):
    B = kraw.shape[0]
    n = kraw.shape[2]
    raw = pl.BlockSpec((1, 1, n, kraw.shape[3]), lambda b, g: (b, g, 0, 0))
    full = lambda a: pl.BlockSpec(a.shape, lambda b, g: (0,) * a.ndim)
    tab = pl.BlockSpec((1, n, LANES), lambda b, g: (b, 0, 0))
    return pl.pallas_call(
        _compress_body,
        grid=(B, 2),
        in_specs=[raw, raw, full(posa), full(posb)] + [full(w) for w in wk] + [full(w) for w in wv] + [tab, tab],
        out_specs=[pl.BlockSpec((1, 1, n, LANES), lambda b, g: (b, g, 0, 0)),
                   pl.BlockSpec((1, 1, LANES, n), lambda b, g: (b, g, 0, 0))],
        out_shape=[jax.ShapeDtypeStruct((B, 2, n, LANES), BF16), jax.ShapeDtypeStruct((B, 2, LANES, n), BF16)],
        compiler_params=_cparams(("parallel", "parallel")),
        name="nsa_compress",
    )(kraw, vraw, posa, posb, *wk, *wv, cos_c, sin_c)


def _split3(p):
    hi = p.astype(BF16)
    r1 = p - hi.astype(F32)
    mid = r1.astype(BF16)
    lo = (r1 - mid.astype(F32)).astype(BF16)
    return hi, mid, lo


def _cmp_select_body(cfg, q_ref, kc_ref, vct_ref, ovt_ref, gate_ref, o_ref, sb_ref):
    tq, n_cmp_pad, n_blk = cfg
    q0 = pl.program_id(1) * tq
    rows = 8 * tq
    half_rows = rows // 2
    tpos = q0 + (lax.broadcasted_iota(jnp.int32, (1, rows), 1) & (tq - 1))
    blk_end = lax.broadcasted_iota(jnp.int32, (n_cmp_pad, 1), 0) * CMP_STRIDE + (CMP_BLOCK - 1)
    cbias = _tile_lanes(jnp.where(blk_end <= tpos[:, :tq], 0.0, NEG_INF), 8)
    sees_any = tpos >= CMP_BLOCK - 1
    top = _first_half_rows()
    jrow = lax.broadcasted_iota(jnp.int32, (n_blk, 1), 0)
    sub8 = lax.broadcasted_iota(jnp.int32, (8, 1), 0)
    cur = (q0 + lax.broadcasted_iota(jnp.int32, (1, tq), 1)) >> 6
    forced = (jrow == 0) | (jrow == cur) | (jrow == cur - 1)
    ovt = ovt_ref[...]

    for gp in range(2):
        lhs = _pair_lhs([q_ref[0, :, (4 * gp + r) * LANES:(4 * gp + r + 1) * LANES] for r in range(4)])
        s = lax.dot_general(kc_ref[0, gp], lhs, _NT, preferred_element_type=F32)
        s = s + cbias
        m = jnp.max(s, 0, keepdims=True)
        p = jnp.exp2(s - m)
        l = jnp.sum(p, 0, keepdims=True)
        pc = p * jnp.where(sees_any, 1.0 / l, 0.0)

        vct = vct_ref[0, gp]
        pb = pc.astype(BF16)
        pv = (jnp.dot(jnp.where(top, vct, 0), pb[:, :half_rows], preferred_element_type=F32)
              + jnp.dot(jnp.where(top, 0, vct), pb[:, half_rows:], preferred_element_type=F32))
        gate_t = gate_ref[0, :, gp * LANES:(gp + 1) * LANES].T
        ga = _gate_rows(gate_t, [2 * r for r in range(4)])
        gb = _gate_rows(gate_t, [2 * r + 1 for r in range(4)])
        out_t = pv * jnp.where(top, ga, gb)
        for r in range(4):
            o_ref[0, :, (4 * gp + r) * LANES:(4 * gp + r + 1) * LANES] = (
                out_t[:, r * tq:(r + 1) * tq].T.astype(o_ref.dtype))

        for half in range(2):
            c0 = 4 * half * tq
            psum = pc[:, c0:c0 + tq] + pc[:, c0 + tq:c0 + 2 * tq] + pc[:, c0 + 2 * tq:c0 + 3 * tq] + pc[:, c0 + 3 * tq:c0 + 4 * tq]
            imp = sum(jnp.dot(ovt, part, preferred_element_type=F32) for part in _split3(psum))[:n_blk]
            score = jnp.where(forced, SEL_FORCE, jnp.where(jrow <= cur, imp, -SEL_FORCE))
            groups = [score[g:g + 8] for g in range(0, n_blk, 8)]
            ranks = [jnp.zeros((8, tq), F32) for _ in groups]
            for j in range(n_blk):
                row = score[j:j + 1, :]
                for gi, grp in enumerate(groups):
                    if j < 8 * gi:
                        beats = row >= grp
                    elif j >= 8 * gi + 8:
                        beats = row > grp
                    else:
                        beats = (row > grp) | ((row == grp) & (sub8 > j - 8 * gi))
                    ranks[gi] = ranks[gi] + jnp.where(beats, 1.0, 0.0)
            rank = jnp.concatenate(ranks, 0)
            bias = jnp.where(rank < min(N_SEL, n_blk), 0.0, NEG_INF)
            bias = jnp.concatenate([bias, jnp.zeros((LANES - n_blk, tq), F32)], 0)
            sb_ref[0, :, (2 * gp + half) * LANES:(2 * gp + half + 1) * LANES] = bias.T.astype(BF16)


def _cmp_select(q, kc, vct, ovt, gate, tq=128):
    B, S, _ = q.shape
    n_cmp_pad = kc.shape[2]
    n_blk = S // SEL_BLOCK
    tok = lambda width: pl.BlockSpec((1, tq, width), lambda b, i: (b, i, 0))
    return pl.pallas_call(
        functools.partial(_cmp_select_body, (tq, n_cmp_pad, n_blk)),
        grid=(B, S // tq),
        in_specs=[tok(MIX_WIDTH), pl.BlockSpec((1, 2, n_cmp_pad, LANES), lambda b, i: (b, 0, 0, 0)),
                  pl.BlockSpec((1, 2, LANES, n_cmp_pad), lambda b, i: (b, 0, 0, 0)),
                  pl.BlockSpec(ovt.shape, lambda b, i: (0, 0)), tok(2 * LANES)],
        out_specs=[tok(MIX_WIDTH), tok(4 * LANES)],
        out_shape=[jax.ShapeDtypeStruct((B, S, MIX_WIDTH), BF16), jax.ShapeDtypeStruct((B, S, 4 * LANES), BF16)],
        compiler_params=_cparams(("parallel", "parallel")),
        name="nsa_cmp_select",
    )(q, kc, vct, ovt, gate)


def _out_body(n_o, *refs):
    o_refs = refs[:n_o]
    z_ref, x_ref, p_ref, wo_ref, g_ref, b_ref, wg_ref, wp_ref, out_ref = refs[n_o:]
    tm = x_ref.shape[1]
    blocks = [slice(r0, r0 + OUT_ROWS) for r0 in range(0, tm, OUT_ROWS)]
    gated = []
    for rs in blocks:
        o = o_refs[0][0, rs].astype(F32)
        for r in o_refs[1:]:
            o = o + r[0, rs].astype(F32)
        gated.append((o * jax.nn.silu(z_ref[0, rs].astype(F32))).astype(BF16))
    y = [jnp.dot(g, wo_ref[...], preferred_element_type=F32) for g in gated]
    emb = [jnp.dot(p_ref[0, 0, rs].astype(BF16), wp_ref[...], preferred_element_type=F32) for rs in blocks]
    xn = []
    for rs, yb in zip(blocks, y):
        u = DEEPNORM_ALPHA * x_ref[0, rs] + yb
        mu = jnp.mean(u, -1, keepdims=True)
        var = jnp.mean(jnp.square(u - mu), -1, keepdims=True)
        xn.append((u - mu) * lax.rsqrt(var + LN_EPS) * g_ref[...] + b_ref[...])
    gate = [jax.nn.sigmoid(jnp.dot(v.astype(BF16), wg_ref[...], preferred_element_type=F32)) for v in xn]
    for rs, v, gt, e in zip(blocks, xn, gate, emb):
        out_ref[0, rs] = v + gt * e


def _out_block(o_list, z, x, p, layer, wo, g, b, wg, wp, tm=512):
    B, S, _ = x.shape
    tok = lambda width: pl.BlockSpec((1, tm, width), lambda bb, i: (bb, i, 0))
    full = lambda a: pl.BlockSpec(a.shape, lambda bb, i: (0,) * a.ndim)
    n_o = len(o_list)
    return pl.pallas_call(
        functools.partial(_out_body, n_o),
        grid=(B, S // tm),
        in_specs=[tok(MIX_WIDTH)] * n_o + [tok(MIX_WIDTH), tok(D_MODEL),
                                           pl.BlockSpec((1, 1, tm, PLE_DIM), lambda bb, i: (layer, bb, i, 0)),
                                           full(wo), full(g), full(b), full(wg), full(wp)],
        out_specs=tok(D_MODEL),
        out_shape=jax.ShapeDtypeStruct((B, S, D_MODEL), F32),
        compiler_params=_cparams(("parallel", "parallel")),
        name="out_block",
    )(*o_list, z, x, p, wo, g, b, wg, wp)


def _swa_weights(w_in):
    q, k, v, z = jnp.split(w_in, [MIX_WIDTH, MIX_WIDTH + KV_WIDTH, MIX_WIDTH + 2 * KV_WIDTH], axis=1)
    return jnp.concatenate([q[:, _QPERM], z[:, _OPERM], k[:, _KPERM], v], 1).astype(BF16)


def _nsa_weights(w_in):
    cuts = np.cumsum([MIX_WIDTH] + [KV_WIDTH] * 6 + [3 * N_HEADS]).tolist()
    q, k_c, v_c, k_s, v_s, k_w, v_w, gl, z = jnp.split(w_in, cuts, axis=1)
    q = q[:, _QPERM]
    gl = jnp.concatenate(
        [jnp.pad(gl[:, np.array([br * N_HEADS + _POS_HEAD[gp * 8 + c] for br in range(3) for c in range(8)])],
                 ((0, 0), (0, LANES - 24))) for gp in range(2)], 1)
    return jnp.concatenate([q, z[:, _OPERM], k_s[:, _KPERM], v_s, k_w[:, _KPERM], v_w, k_c, v_c, gl], 1).astype(BF16)


def _phi_weights(w1, w2, rope_layout):
    h = w1.shape[1]
    w1 = w1.reshape(2, CMP_STRIDE, HEAD_DIM, h)
    eye = jnp.eye(2, dtype=w1.dtype)
    w1a, w1b = (jnp.einsum("ldj,gh->lgdhj", w1[t], eye).reshape(CMP_STRIDE * 2 * HEAD_DIM, 2 * h) for t in range(2))
    if rope_layout:
        w2 = jnp.einsum("jpi,gh->gjphi", w2.reshape(h, 2, 32), eye).reshape(2 * h, LANES)
    else:
        w2 = jnp.einsum("jd,gh->gjhd", w2, eye).reshape(2 * h, LANES)
    return w1a.astype(BF16), w1b.astype(BF16), w2.astype(BF16)


def _overlap_matrix_t(n_cmp, n_cmp_pad, n_blk):
    cstart = np.arange(n_cmp_pad) * CMP_STRIDE
    j = np.arange(LANES)
    ov = (cstart[None, :] < (j[:, None] + 1) * SEL_BLOCK) & (cstart[None, :] + CMP_BLOCK > j[:, None] * SEL_BLOCK)
    ov &= (np.arange(n_cmp_pad)[None, :] < n_cmp) & (j[:, None] < n_blk)
    return jnp.asarray(ov.astype(np.float32), dtype=BF16)


def _layer_swa(x, tabs64, w_in, sinks):
    q, z, k, vt = _proj_gqa(x, _swa_weights(w_in), *tabs64, n_kv=1, with_cmp=False)
    o = _flash_gqa(q, k, vt, mode="band", window=SWA_WINDOW, sinks=sinks.astype(F32), nsub=4)
    return [o], z, _OPERM


def _layer_mla(x, tabs32, w_in, q_norm, kv_norm, w_uq, w_ukv):
    lat = MLA_Q_LORA + MLA_KV_LORA
    w1 = jnp.concatenate([w_in[:, :lat], w_in[:, lat:lat + MLA_ROPE][:, _KRPERM], w_in[:, lat + MLA_ROPE:]], 1)
    wkv = jnp.concatenate([w_ukv[:, _KNPERM], w_ukv[:, _VPERM]], 1)
    q, k, vt, kr, z = _proj_mla(x, w1.astype(BF16), q_norm[None, :], kv_norm[None, :],
                                w_uq[:, _UQPERM].astype(BF16), wkv.astype(BF16), *tabs32)
    return [_flash_mla(q, k, vt, kr)], z, None


def _layer_nsa(x, tabs64, positions, w_in, cmp_pos, phi_k1, phi_k2, phi_v1, phi_v2):
    B, S, _ = x.shape
    q, z, ks, vst, kw, vwt, kraw, vraw, gate = _proj_gqa(x, _nsa_weights(w_in), *tabs64, n_kv=2, with_cmp=True)
    n_chunk = S // CMP_STRIDE
    n_cmp = (S - CMP_BLOCK) // CMP_STRIDE + 1
    kraw = kraw.reshape(B, 2, n_chunk, CMP_STRIDE * LANES)
    vraw = vraw.reshape(B, 2, n_chunk, CMP_STRIDE * LANES)
    pos2 = jnp.tile(cmp_pos.reshape(2, CMP_STRIDE, 1, HEAD_DIM), (1, 1, 2, 1)).reshape(2, 1, CMP_STRIDE * LANES)
    end = jnp.minimum(jnp.arange(n_chunk) * CMP_STRIDE + CMP_BLOCK - 1, S - 1)
    cos_c, sin_c = _rope_tables(positions[:, end].reshape(-1), HEAD_DIM // 2)
    kc, vct = _compress(kraw, vraw, pos2[0], pos2[1], _phi_weights(phi_k1, phi_k2, True),
                        _phi_weights(phi_v1, phi_v2, False),
                        cos_c.reshape(B, n_chunk, LANES), sin_c.reshape(B, n_chunk, LANES))
    ovt = _overlap_matrix_t(n_cmp, n_chunk, S // SEL_BLOCK)
    o_cmp, sel = _cmp_select(q, kc, vct, ovt, gate)
    o_slc = _flash_gqa(q, ks, vst, mode="causal", sel=sel, gate=gate, gate_base=8, tq=256)
    o_win = _flash_gqa(q, kw, vwt, mode="band", window=NSA_WINDOW, gate=gate, gate_base=16, nsub=2)
    return [o_cmp, o_slc, o_win], z, _OPERM


def kernel(x, p, positions, w_out, ln_g, ln_b, pe_gate, pe_proj, l0_w_in, l0_sinks, l1_w_in, l1_q_norm, l1_kv_norm, l1_w_uq, l1_w_ukv, l2_w_in, l2_cmp_pos, l2_phi_k1, l2_phi_k2, l2_phi_v1, l2_phi_v2, l3_w_in, l3_sinks):
    B, S, _ = x.shape
    pos_flat = positions.reshape(-1)
    tabs64 = [t.reshape(B, S, LANES) for t in _rope_tables(pos_flat, HEAD_DIM // 2)]
    tabs32 = [t.reshape(B, S, LANES) for t in _rope_tables(pos_flat, MLA_ROPE // 2)]
    layers = (
        lambda h: _layer_swa(h, tabs64, l0_w_in, l0_sinks),
        lambda h: _layer_mla(h, tabs32, l1_w_in, l1_q_norm, l1_kv_norm, l1_w_uq, l1_w_ukv),
        lambda h: _layer_nsa(h, tabs64, positions, l2_w_in, l2_cmp_pos, l2_phi_k1, l2_phi_k2, l2_phi_v1, l2_phi_v2),
        lambda h: _layer_swa(h, tabs64, l3_w_in, l3_sinks),
    )
    for i, layer in enumerate(layers):
        o_list, z, operm = layer(x)
        wo = w_out[i] if operm is None else w_out[i][operm, :]
        x = _out_block(o_list, z, x, p, i, wo.astype(BF16), ln_g[i][None, :], ln_b[i][None, :],
                       pe_gate[i].astype(BF16), pe_proj[i].astype(BF16))
    return x
```

```python
import functools

import numpy as np
import jax
import jax.numpy as jnp
from jax import lax
from jax.experimental import pallas as pl
from jax.experimental.pallas import tpu as pltpu

F32 = jnp.float32
BF16 = jnp.bfloat16

LANES = 128
VMEM_LIMIT = 56 * 1024 * 1024

D_MODEL = 1024
DEPTH = 4
HEAD_DIM = 64
N_HEADS = 16
MIX_WIDTH = 1024
ROPE_THETA = 10000.0
NEG_INF = -1e30
PLE_DIM = 256
DEEPNORM_ALPHA = (2 * DEPTH) ** 0.25
LN_EPS = 1e-5
RMS_EPS = 1e-6

SWA_WINDOW = 128
KV_GROUPS = 4
KV_WIDTH = KV_GROUPS * HEAD_DIM

MLA_Q_LORA = 384
MLA_KV_LORA = 256
MLA_NOPE = 64
MLA_ROPE = 32
MLA_V = 64

CMP_BLOCK = 32
CMP_STRIDE = 16
SEL_BLOCK = 64
N_SEL = 8
SEL_FORCE = 1e9
NSA_WINDOW = 512
PHI_HIDDEN = 256

BAND_BLOCK = 128
OUT_ROWS = 128
LOG2E = 1.4426950408889634


def _cparams(sem, flags=None):
    return pltpu.CompilerParams(dimension_semantics=sem, vmem_limit_bytes=VMEM_LIMIT, flags=flags)


_ATTN_FLAGS = None


def _pair_head(gp, r, half):
    return (2 * gp + half) * 4 + r


def _gqa_perms():
    qperm, kperm, operm = [], [], []
    for gp in range(2):
        for r in range(4):
            for part in range(2):
                for half in range(2):
                    h = _pair_head(gp, r, half)
                    qperm += [h * 64 + part * 32 + i for i in range(32)]
            for half in range(2):
                h = _pair_head(gp, r, half)
                operm += [h * 64 + d for d in range(64)]
        for part in range(2):
            for half in range(2):
                g = 2 * gp + half
                kperm += [g * 64 + part * 32 + i for i in range(32)]
    return np.array(qperm), np.array(kperm), np.array(operm)


_QPERM, _KPERM, _OPERM = _gqa_perms()
_POS_HEAD = np.array([_pair_head(gp, r, half) for gp in range(2) for r in range(4) for half in range(2)])


def _mla_perms():
    uq, kr = [], []
    for hg in range(4):
        for a in range(4):
            uq += [(4 * hg + a) * 96 + d for d in range(64)]
        for part in range(2):
            for a in range(4):
                uq += [(4 * hg + a) * 96 + 64 + part * 16 + i for i in range(16)]
    for part in range(2):
        for _ in range(4):
            kr += [part * 16 + i for i in range(16)]
    kn = [h * 128 + d for h in range(16) for d in range(64)]
    vv = [h * 128 + 64 + d for h in range(16) for d in range(64)]
    return np.array(uq), np.array(kr), np.array(kn), np.array(vv)


_UQPERM, _KRPERM, _KNPERM, _VPERM = _mla_perms()


def _rope_lane_tables(half):
    inv = ROPE_THETA ** (-jnp.arange(half, dtype=F32) / half)
    lane = np.arange(LANES)
    inv_l = inv[lane % half][None, :]
    sign = np.where(lane < 64, -1.0, 1.0).astype(np.float32)[None, :]
    return inv_l, jnp.asarray(sign)


def _rope_table_body(pos_ref, inv_ref, sign_ref, cos_ref, sin_ref):
    ang = pos_ref[...].astype(F32) * inv_ref[...]
    cos_ref[...] = jnp.cos(ang)
    sin_ref[...] = jnp.sin(ang) * sign_ref[...]


def _rope_tables(pos, half):
    n = pos.shape[0]
    tr = min(n, 1024)
    inv_l, sign = _rope_lane_tables(half)
    row = pl.BlockSpec((tr, LANES), lambda i: (i, 0))
    cst = pl.BlockSpec((1, LANES), lambda i: (0, 0))
    return pl.pallas_call(
        _rope_table_body,
        grid=(n // tr,),
        in_specs=[pl.BlockSpec((tr, 1), lambda i: (i, 0)), cst, cst],
        out_specs=[row, row],
        out_shape=[jax.ShapeDtypeStruct((n, LANES), F32)] * 2,
        compiler_params=_cparams(("parallel",)),
        name="rope_tables",
    )(pos.reshape(n, 1), inv_l, sign)


def _rope_cols(v, cos, sin):
    return v * cos + pltpu.roll(v, 64, 1) * sin


def _proj_gqa_body(n_kv, with_cmp, x_ref, w_ref, cos_ref, sin_ref, *outs):
    xb = x_ref[0].astype(BF16)
    cos = cos_ref[0]
    sin = sin_ref[0]

    def mm(c0, width):
        return jnp.dot(xb, w_ref[:, c0:c0 + width], preferred_element_type=F32)

    q_ref, z_ref = outs[0], outs[1]
    for c in range(0, MIX_WIDTH, 512):
        acc = mm(c, 512) * (HEAD_DIM ** -0.5 * LOG2E)
        for j in range(0, 512, LANES):
            q_ref[0, :, c + j:c + j + LANES] = _rope_cols(acc[:, j:j + LANES], cos, sin).astype(BF16)
    for c in range(0, MIX_WIDTH, 512):
        z_ref[0, :, c:c + 512] = mm(MIX_WIDTH + c, 512).astype(BF16)
    col = 2 * MIX_WIDTH
    o = 2
    for _ in range(n_kv):
        acc = mm(col, 2 * KV_WIDTH)
        for j in range(0, KV_WIDTH, LANES):
            outs[o][0, :, j:j + LANES] = _rope_cols(acc[:, j:j + LANES], cos, sin).astype(BF16)
        outs[o + 1][0] = acc[:, KV_WIDTH:].T.astype(BF16)
        col += 2 * KV_WIDTH
        o += 2
    if with_cmp:
        acc = mm(col, 2 * KV_WIDTH)
        for t in range(2):
            for gp in range(2):
                c0 = t * KV_WIDTH + gp * LANES
                outs[o + t][0, gp] = acc[:, c0:c0 + LANES]
        col += 2 * KV_WIDTH
        outs[o + 2][0] = jax.nn.sigmoid(mm(col, 2 * LANES))


def _proj_gqa(x, w, cos, sin, n_kv, with_cmp, tm=512):
    B, S, _ = x.shape
    n = w.shape[1]
    tok = lambda width: pl.BlockSpec((1, tm, width), lambda b, i: (b, i, 0))
    out_specs = [tok(MIX_WIDTH), tok(MIX_WIDTH)]
    out_shape = [jax.ShapeDtypeStruct((B, S, MIX_WIDTH), BF16)] * 2
    for _ in range(n_kv):
        out_specs += [tok(KV_WIDTH), pl.BlockSpec((1, KV_WIDTH, tm), lambda b, i: (b, 0, i))]
        out_shape += [jax.ShapeDtypeStruct((B, S, KV_WIDTH), BF16), jax.ShapeDtypeStruct((B, KV_WIDTH, S), BF16)]
    if with_cmp:
        raw = pl.BlockSpec((1, 2, tm, LANES), lambda b, i: (b, 0, i, 0))
        out_specs += [raw, raw, tok(2 * LANES)]
        out_shape += [jax.ShapeDtypeStruct((B, 2, S, LANES), F32)] * 2
        out_shape += [jax.ShapeDtypeStruct((B, S, 2 * LANES), F32)]
    return pl.pallas_call(
        functools.partial(_proj_gqa_body, n_kv, with_cmp),
        grid=(B, S // tm),
        in_specs=[tok(D_MODEL), pl.BlockSpec((D_MODEL, n), lambda b, i: (0, 0)), tok(LANES), tok(LANES)],
        out_specs=out_specs,
        out_shape=out_shape,
        compiler_params=_cparams(("parallel", "parallel")),
        name="proj_gqa",
    )(x, w, cos, sin)


def _rms(v, g):
    return v * lax.rsqrt(jnp.mean(v * v, -1, keepdims=True) + RMS_EPS) * g


def _proj_mla_body(x_ref, w1_ref, qn_ref, kvn_ref, wuq_ref, wkv_ref, cos_ref, sin_ref,
                   q_ref, k_ref, v_ref, kr_ref, z_ref):
    xb = x_ref[0].astype(BF16)
    cos = cos_ref[0]
    sin = sin_ref[0]
    lat = MLA_Q_LORA + MLA_KV_LORA
    c = jnp.dot(xb, w1_ref[:, :lat + LANES], preferred_element_type=F32)
    cq = _rms(c[:, :MLA_Q_LORA], qn_ref[...]).astype(BF16)
    ckv = _rms(c[:, MLA_Q_LORA:lat], kvn_ref[...]).astype(BF16)
    kr_ref[0] = _rope_cols(c[:, lat:lat + LANES], cos, sin).astype(BF16)
    scale = (MLA_NOPE + MLA_ROPE) ** -0.5 * LOG2E
    for hg in range(4):
        c0 = hg * 384
        q = jnp.dot(cq, wuq_ref[:, c0:c0 + 384], preferred_element_type=F32) * scale
        q_ref[0, :, c0:c0 + 256] = q[:, :256].astype(BF16)
        q_ref[0, :, c0 + 256:c0 + 384] = _rope_cols(q[:, 256:], cos, sin).astype(BF16)
    for c0 in range(0, MIX_WIDTH, 512):
        k_ref[0, :, c0:c0 + 512] = jnp.dot(ckv, wkv_ref[:, c0:c0 + 512], preferred_element_type=F32).astype(BF16)
        v_ref[0, c0:c0 + 512, :] = jnp.dot(ckv, wkv_ref[:, MIX_WIDTH + c0:MIX_WIDTH + c0 + 512],
                                            preferred_element_type=F32).T.astype(BF16)
        z_ref[0, :, c0:c0 + 512] = jnp.dot(xb, w1_ref[:, lat + LANES + c0:lat + LANES + c0 + 512],
                                            preferred_element_type=F32).astype(BF16)


def _proj_mla(x, w1, qn, kvn, wuq, wkv, cos, sin, tm=512):
    B, S, _ = x.shape
    tok = lambda width: pl.BlockSpec((1, tm, width), lambda b, i: (b, i, 0))
    full = lambda a: pl.BlockSpec(a.shape, lambda b, i: (0,) * a.ndim)
    widths = [16 * 96, MIX_WIDTH, MIX_WIDTH, LANES, MIX_WIDTH]
    out_specs = [tok(w) for w in widths]
    out_shape = [jax.ShapeDtypeStruct((B, S, w), BF16) for w in widths]
    out_specs[2] = pl.BlockSpec((1, MIX_WIDTH, tm), lambda b, i: (b, 0, i))
    out_shape[2] = jax.ShapeDtypeStruct((B, MIX_WIDTH, S), BF16)
    return pl.pallas_call(
        _proj_mla_body,
        grid=(B, S // tm),
        in_specs=[tok(D_MODEL), full(w1), full(qn), full(kvn), full(wuq), full(wkv), tok(LANES), tok(LANES)],
        out_specs=out_specs,
        out_shape=out_shape,
        compiler_params=_cparams(("parallel", "parallel")),
        name="proj_mla",
    )(x, w1, qn, kvn, wuq, wkv, cos, sin)


_NT = (((1,), (1,)), ((), ()))


def _lane_iota():
    return lax.broadcasted_iota(jnp.int32, (1, LANES), 1)


def _first_half_rows():
    return lax.broadcasted_iota(jnp.int32, (LANES, 1), 0) < 64


def _pair_lhs(q_cols, extra_a=None, extra_b=None):
    m_a = ((_lane_iota() >> 5) & 1) == 0
    a_rows, b_rows = [], []
    for qc in q_cols:
        a = jnp.where(m_a, qc, 0)
        b = jnp.where(m_a, 0, qc)
        if extra_a is not None:
            a = jnp.concatenate([a, extra_a], 1)
            b = jnp.concatenate([b, extra_b], 1)
        a_rows.append(a)
        b_rows.append(b)
    return jnp.concatenate(a_rows + b_rows, 0)


def _tile_lanes(bias, reps):
    return jnp.concatenate([bias] * reps, 1)


def _online_step(lhs, kt, vt, bias, state):
    return _softmax_pv(_scores(lhs, kt), vt, bias, state)


def _scores(lhs, kt):
    return lax.dot_general(kt, lhs, _NT, preferred_element_type=F32)


def _softmax_pv(s, vt, bias, state):
    m, acc_a, acc_b = state
    half = s.shape[1] // 2
    if bias is not None:
        s = s + bias
    m_new = jnp.maximum(m, jnp.max(s, 0, keepdims=True))
    p = jnp.exp2(s - m_new).astype(BF16)
    top = _first_half_rows()
    pv_a = jnp.dot(jnp.where(top, vt, 1), p[:, :half], preferred_element_type=F32)
    pv_b = jnp.dot(jnp.where(top, 1, vt), p[:, half:], preferred_element_type=F32)
    if acc_a is None:
        return m_new, pv_a, pv_b
    alpha = jnp.exp2(m - m_new)
    return m_new, acc_a * alpha[:, :half] + pv_a, acc_b * alpha[:, half:] + pv_b


def _causal_sweep(score_fn, pv_fn, n_chain, n_full, tk, bias, init, buf_a, buf_b):
    def scores_to(buf, start, bias=None):
        start = pl.multiple_of(start, BAND_BLOCK)
        for c in range(n_chain):
            s = score_fn(c, start)
            buf[c] = s if bias is None else s + bias

    def consume(buf, start, states):
        start = pl.multiple_of(start, BAND_BLOCK)
        return tuple(pv_fn(c, buf[c], start, states[c]) for c in range(n_chain))

    diag_start = n_full * tk
    scores_to(buf_a, diag_start, bias)

    def two_tiles(i, carry):
        states, prev = carry
        scores_to(buf_b, 2 * i * tk)
        states = consume(buf_a, prev, states)
        scores_to(buf_a, (2 * i + 1) * tk)
        states = consume(buf_b, 2 * i * tk, states)
        return states, (2 * i + 1) * tk

    states, prev = lax.fori_loop(0, n_full // 2, two_tiles, (init, diag_start))

    def odd_tail(states):
        last = (n_full - 1) * tk
        scores_to(buf_b, last)
        return consume(buf_b, last, consume(buf_a, prev, states))

    return lax.cond(n_full % 2 == 1, odd_tail, lambda st: consume(buf_a, prev, st), states)


def _finish(state, gate_rows):
    _, acc_a, acc_b = state
    half = acc_a.shape[1]
    inv_a = 1.0 / acc_a[64:65, :]
    inv_b = 1.0 / acc_b[0:1, :]
    if gate_rows is not None:
        inv_a = inv_a * gate_rows[:, :half]
        inv_b = inv_b * gate_rows[:, half:]
    return jnp.where(_first_half_rows(), acc_a * inv_a, acc_b * inv_b)


def _gate_rows(gate_t, cols):
    return jnp.concatenate([gate_t[c:c + 1, :] for c in cols], 1)


def _flash_gqa_body(cfg, *refs):
    tq, nsub, n_prev, mode, use_sel, use_sink, gate_base, tk = cfg
    refs = list(refs)
    q_ref, k_ref, vt_ref, bias_ref = refs[:4]
    nxt = 4
    sb_ref = gate_ref = sink_ref = None
    if use_sel:
        sb_ref = refs[nxt]; nxt += 1
    if gate_base is not None:
        gate_ref = refs[nxt]; nxt += 1
    if use_sink:
        sink_ref = refs[nxt]; nxt += 1
    o_ref = refs[nxt]

    step_i = pl.program_id(1)
    rows = 8 * tq
    top = _first_half_rows()
    chains = [(sub, gp) for sub in range(nsub) for gp in range(2)]
    lhs, init = [], []
    for sub, gp in chains:
        tok = slice(sub * tq, (sub + 1) * tq)
        q_cols = [q_ref[0, tok, (4 * gp + r) * LANES:(4 * gp + r + 1) * LANES] for r in range(4)]
        if use_sel:
            lhs.append(_pair_lhs(q_cols, sb_ref[0, tok, 2 * gp * LANES:(2 * gp + 1) * LANES],
                                 sb_ref[0, tok, (2 * gp + 1) * LANES:(2 * gp + 2) * LANES]))
        else:
            lhs.append(_pair_lhs(q_cols))
        if use_sink:
            m0 = jnp.concatenate([jnp.full((1, tq), sink_ref[_pair_head(gp, r, half)] * LOG2E, F32)
                                  for half in range(2) for r in range(4)], 1)
            init.append((m0, jnp.where(top, jnp.zeros((LANES, rows // 2), F32), 1.0),
                         jnp.where(top, jnp.ones((LANES, rows // 2), F32), 0.0)))
        else:
            init.append((jnp.full((1, rows), NEG_INF, F32), None, None))

    def step(starts, states, biases):
        starts = [pl.multiple_of(st, BAND_BLOCK) for st in starts]
        s = []
        for c, (sub, gp) in enumerate(chains):
            kt = k_ref[0, pl.ds(starts[c], tk), gp * LANES:(gp + 1) * LANES]
            if use_sel:
                blk = (starts[c] + lax.broadcasted_iota(jnp.int32, (tk, LANES), 0)) >> 6
                onehot = (blk == lax.broadcasted_iota(jnp.int32, (tk, LANES), 1)).astype(BF16)
                kt = jnp.concatenate([kt, onehot], 1)
            s.append(_scores(lhs[c], kt))
        if mode == "band":
            buf = refs[nxt + 1]
            for c in range(len(chains)):
                buf[c] = s[c]
            s = [buf[c] for c in range(len(chains))]
        return tuple(
            _softmax_pv(s[c], vt_ref[0, gp * LANES:(gp + 1) * LANES, pl.ds(starts[c], tk)], biases[c], states[c])
            for c, (sub, gp) in enumerate(chains))

    if mode == "band":
        tiles = [step_i * nsub + sub for sub, gp in chains]
        states = step([jnp.maximum(t * tq - (tk - tq), 0) for t in tiles], init,
                      [_tile_lanes(bias_ref[jnp.minimum(t, n_prev)], 8) for t in tiles])
    else:
        buf_a, buf_b = refs[nxt + 1], refs[nxt + 2]

        def score_fn(c, start):
            gp = chains[c][1]
            kt = k_ref[0, pl.ds(start, tk), gp * LANES:(gp + 1) * LANES]
            if use_sel:
                blk = (start + lax.broadcasted_iota(jnp.int32, (tk, LANES), 0)) >> 6
                onehot = (blk == lax.broadcasted_iota(jnp.int32, (tk, LANES), 1)).astype(BF16)
                kt = jnp.concatenate([kt, onehot], 1)
            return _scores(lhs[c], kt)

        def pv_fn(c, s, start, state):
            gp = chains[c][1]
            return _softmax_pv(s, vt_ref[0, gp * LANES:(gp + 1) * LANES, pl.ds(start, tk)], None, state)

        zeros = jnp.zeros((LANES, rows // 2), F32)
        states = _causal_sweep(score_fn, pv_fn, len(chains), (step_i * tq) // tk, tk,
                               _tile_lanes(bias_ref[step_i % (tk // tq)], 8),
                               tuple((m0, zeros, zeros) for m0, _, _ in init), buf_a, buf_b)

    for c, (sub, gp) in enumerate(chains):
        tok = slice(sub * tq, (sub + 1) * tq)
        gate_rows = None
        if gate_base is not None:
            gate_t = gate_ref[0, tok, gp * LANES:(gp + 1) * LANES].T
            gate_rows = _gate_rows(gate_t, [gate_base + 2 * r for r in range(4)]
                                   + [gate_base + 2 * r + 1 for r in range(4)])
        out_t = _finish(states[c], gate_rows)
        for r in range(4):
            o_ref[0, tok, (4 * gp + r) * LANES:(4 * gp + r + 1) * LANES] = (
                out_t[:, r * tq:(r + 1) * tq].T.astype(o_ref.dtype))


def _band_bias(window, tq, tk):
    pad = tk - tq
    r = np.arange(tk)[:, None]
    c = np.arange(tq)[None, :]
    out = []
    for v in range(pad // tq + 1):
        d = r - c - (v * tq if v < pad // tq else pad)
        out.append(np.where((d <= 0) & (d > -window), 0.0, NEG_INF))
    return jnp.asarray(np.stack(out), F32)


def _diag_bias(tq, tk):
    r = np.arange(tk)[:, None]
    c = np.arange(tq)[None, :]
    return jnp.asarray(np.stack([np.where(r <= v * tq + c, 0.0, NEG_INF) for v in range(tk // tq)]), F32)


def _flash_gqa(q, k, vt, *, mode, window=None, sel=None, gate=None, gate_base=None, sinks=None,
               tq=128, nsub=1, tk=512):
    B, S, _ = q.shape
    n_prev = 0
    if mode == "band":
        pad = -(-(window - 1) // BAND_BLOCK) * BAND_BLOCK
        tk = pad + tq
        bias = _band_bias(window, tq, tk)
        n_prev = pad // tq
    else:
        assert nsub == 1
        bias = _diag_bias(tq, tk)
    bias_spec = pl.BlockSpec(bias.shape, lambda b, i: (0, 0, 0))
    cfg = (tq, nsub, n_prev, mode, sel is not None, sinks is not None, gate_base if gate is not None else None, tk)
    tok = lambda width: pl.BlockSpec((1, nsub * tq, width), lambda b, i: (b, i, 0))
    in_specs = [tok(MIX_WIDTH), pl.BlockSpec((1, S, KV_WIDTH), lambda b, i: (b, 0, 0)),
                pl.BlockSpec((1, KV_WIDTH, S), lambda b, i: (b, 0, 0)), bias_spec]
    args = [q, k, vt, bias]
    if sel is not None:
        in_specs.append(tok(4 * LANES))
        args.append(sel)
    if gate is not None:
        in_specs.append(tok(2 * LANES))
        args.append(gate)
    if sinks is not None:
        in_specs.append(pl.BlockSpec(memory_space=pltpu.SMEM))
        args.append(sinks)
    return pl.pallas_call(
        functools.partial(_flash_gqa_body, cfg),
        grid=(B, S // (nsub * tq)),
        in_specs=in_specs,
        out_specs=tok(MIX_WIDTH),
        out_shape=jax.ShapeDtypeStruct((B, S, MIX_WIDTH), BF16),
        scratch_shapes=([pltpu.VMEM((2 * nsub, tk, 8 * tq), F32)] if mode == "band"
                        else [pltpu.VMEM((2, tk, 8 * tq), F32)] * 2),
        compiler_params=_cparams(("parallel", "arbitrary"), _ATTN_FLAGS),
        name=f"attn_{mode}{'_sel' if sel is not None else ''}{window or ''}",
    )(*args)


def _flash_mla_body(cfg, q_ref, k_ref, vt_ref, kr_ref, bias_ref, o_ref, buf_a, buf_b):
    tq, nhg, tk = cfg
    step_i = pl.program_id(2)
    q0 = step_i * tq
    lane = _lane_iota()
    lo = lane < 64
    slot = (lane >> 4) & 3
    n_pair = 2 * nhg
    lhs = []
    for hg in range(nhg):
        qr = q_ref[0, :, hg * 384 + 256:hg * 384 + 384]
        for pair in range(2):
            qn = q_ref[0, :, hg * 384 + pair * LANES:hg * 384 + (pair + 1) * LANES]
            a = jnp.concatenate([jnp.where(lo, qn, 0), jnp.where(slot == 2 * pair, qr, 0)], 1)
            b = jnp.concatenate([jnp.where(lo, 0, qn), jnp.where(slot == 2 * pair + 1, qr, 0)], 1)
            lhs.append(jnp.concatenate([a, b], 0))
    rows = 2 * tq
    init = (jnp.full((1, rows), NEG_INF, F32), jnp.zeros((LANES, tq), F32), jnp.zeros((LANES, tq), F32))

    def score_fn(c, start):
        kt = jnp.concatenate([k_ref[0, pl.ds(start, tk), c * LANES:(c + 1) * LANES],
                              kr_ref[0, pl.ds(start, tk), :]], 1)
        return _scores(lhs[c], kt)

    def pv_fn(c, s, start, state):
        return _softmax_pv(s, vt_ref[0, c * LANES:(c + 1) * LANES, pl.ds(start, tk)], None, state)

    states = _causal_sweep(score_fn, pv_fn, n_pair, q0 // tk, tk, _tile_lanes(bias_ref[step_i % (tk // tq)], 2),
                           (init,) * n_pair, buf_a, buf_b)
    for c in range(n_pair):
        o_ref[0, :, c * LANES:(c + 1) * LANES] = _finish(states[c], None).T.astype(o_ref.dtype)


def _flash_mla(q, k, vt, kr, tq=512, tk=512, nhg=2):
    B, S, _ = q.shape
    bias = _diag_bias(tq, tk)
    return pl.pallas_call(
        functools.partial(_flash_mla_body, (tq, nhg, tk)),
        grid=(B, 4 // nhg, S // tq),
        in_specs=[pl.BlockSpec((1, tq, nhg * 384), lambda b, g, i: (b, i, g)),
                  pl.BlockSpec((1, S, nhg * 256), lambda b, g, i: (b, 0, g)),
                  pl.BlockSpec((1, nhg * 256, S), lambda b, g, i: (b, g, 0)),
                  pl.BlockSpec((1, S, LANES), lambda b, g, i: (b, 0, 0)),
                  pl.BlockSpec(bias.shape, lambda b, g, i: (0, 0, 0))],
        out_specs=pl.BlockSpec((1, tq, nhg * 256), lambda b, g, i: (b, i, g)),
        out_shape=jax.ShapeDtypeStruct((B, S, MIX_WIDTH), BF16),
        scratch_shapes=[pltpu.VMEM((2 * nhg, tk, 2 * tq), F32)] * 2,
        compiler_params=_cparams(("parallel", "parallel", "arbitrary"), _ATTN_FLAGS),
        name="attn_mla",
    )(q, k, vt, kr, bias)


def _compress_body(kraw_ref, vraw_ref, posa_ref, posb_ref, wk1a_ref, wk1b_ref, wk2_ref,
                   wv1a_ref, wv1b_ref, wv2_ref, cos_ref, sin_ref, kc_ref, vct_ref):
    n = kraw_ref.shape[2] // CMP_STRIDE

    def chunks(ref):
        return jnp.concatenate([ref[0, 0, pl.ds(t, n, stride=CMP_STRIDE), :] for t in range(CMP_STRIDE)], 1)

    def phi(raw, w1a, w1b, w2):
        a = jnp.dot((raw + posa_ref[...]).astype(BF16), w1a[...], preferred_element_type=F32)
        b = jnp.dot((raw + posb_ref[...]).astype(BF16), w1b[...], preferred_element_type=F32)
        h = a + pltpu.roll(b, n - 1, 0)
        return jnp.dot(jax.nn.silu(h).astype(BF16), w2[...], preferred_element_type=F32)

    kc = phi(chunks(kraw_ref), wk1a_ref, wk1b_ref, wk2_ref)
    kc_ref[0, 0] = _rope_cols(kc, cos_ref[0], sin_ref[0]).astype(BF16)
    vct_ref[0, 0] = phi(chunks(vraw_ref), wv1a_ref, wv1b_ref, wv2_ref).T.astype(BF16)


def _compress(kraw, vraw, posa, posb, wk, wv, cos_c, sin_c):
    B, _, S, _ = kraw.shape
    n = S // CMP_STRIDE
    raw = pl.BlockSpec((1, 1, S, LANES), lambda b, g: (b, g, 0, 0))
    full = lambda a: pl.BlockSpec(a.shape, lambda b, g: (0,) * a.ndim)
    tab = pl.BlockSpec((1, n, LANES), lambda b, g: (b, 0, 0))
    return pl.pallas_call(
        _compress_body,
        grid=(B, 2),
        in_specs=[raw, raw, full(posa), full(posb)] + [full(w) for w in wk] + [full(w) for w in wv] + [tab, tab],
        out_specs=[pl.BlockSpec((1, 1, n, LANES), lambda b, g: (b, g, 0, 0)),
                   pl.BlockSpec((1, 1, LANES, n), lambda b, g: (b, g, 0, 0))],
        out_shape=[jax.ShapeDtypeStruct((B, 2, n, LANES), BF16), jax.ShapeDtypeStruct((B, 2, LANES, n), BF16)],
        compiler_params=_cparams(("parallel", "parallel")),
        name="nsa_compress",
    )(kraw, vraw, posa, posb, *wk, *wv, cos_c, sin_c)


def _split3(p):
    hi = p.astype(BF16)
    r1 = p - hi.astype(F32)
    mid = r1.astype(BF16)
    lo = (r1 - mid.astype(F32)).astype(BF16)
    return hi, mid, lo


def _cmp_select_body(cfg, q_ref, kc_ref, vct_ref, ovt_ref, gate_ref, o_ref, sb_ref):
    tq, n_cmp_pad, n_blk = cfg
    q0 = pl.program_id(1) * tq
    rows = 8 * tq
    half_rows = rows // 2
    tpos = q0 + (lax.broadcasted_iota(jnp.int32, (1, rows), 1) & (tq - 1))
    blk_end = lax.broadcasted_iota(jnp.int32, (n_cmp_pad, 1), 0) * CMP_STRIDE + (CMP_BLOCK - 1)
    cbias = _tile_lanes(jnp.where(blk_end <= tpos[:, :tq], 0.0, NEG_INF), 8)
    sees_any = tpos >= CMP_BLOCK - 1
    top = _first_half_rows()
    jrow = lax.broadcasted_iota(jnp.int32, (n_blk, 1), 0)
    sub8 = lax.broadcasted_iota(jnp.int32, (8, 1), 0)
    cur = (q0 + lax.broadcasted_iota(jnp.int32, (1, tq), 1)) >> 6
    forced = (jrow == 0) | (jrow == cur) | (jrow == cur - 1)
    ovt = ovt_ref[...]

    for gp in range(2):
        lhs = _pair_lhs([q_ref[0, :, (4 * gp + r) * LANES:(4 * gp + r + 1) * LANES] for r in range(4)])
        s = lax.dot_general(kc_ref[0, gp], lhs, _NT, preferred_element_type=F32)
        s = s + cbias
        m = jnp.max(s, 0, keepdims=True)
        p = jnp.exp2(s - m)
        l = jnp.sum(p, 0, keepdims=True)
        pc = p * jnp.where(sees_any, 1.0 / l, 0.0)

        vct = vct_ref[0, gp]
        pb = pc.astype(BF16)
        pv = (jnp.dot(jnp.where(top, vct, 0), pb[:, :half_rows], preferred_element_type=F32)
              + jnp.dot(jnp.where(top, 0, vct), pb[:, half_rows:], preferred_element_type=F32))
        gate_t = gate_ref[0, :, gp * LANES:(gp + 1) * LANES].T
        ga = _gate_rows(gate_t, [2 * r for r in range(4)])
        gb = _gate_rows(gate_t, [2 * r + 1 for r in range(4)])
        out_t = pv * jnp.where(top, ga, gb)
        for r in range(4):
            o_ref[0, :, (4 * gp + r) * LANES:(4 * gp + r + 1) * LANES] = (
                out_t[:, r * tq:(r + 1) * tq].T.astype(o_ref.dtype))

        for half in range(2):
            c0 = 4 * half * tq
            psum = pc[:, c0:c0 + tq] + pc[:, c0 + tq:c0 + 2 * tq] + pc[:, c0 + 2 * tq:c0 + 3 * tq] + pc[:, c0 + 3 * tq:c0 + 4 * tq]
            imp = sum(jnp.dot(ovt, part, preferred_element_type=F32) for part in _split3(psum))[:n_blk]
            score = jnp.where(forced, SEL_FORCE, jnp.where(jrow <= cur, imp, -SEL_FORCE))
            groups = [score[g:g + 8] for g in range(0, n_blk, 8)]
            ranks = [jnp.zeros((8, tq), F32) for _ in groups]
            for j in range(n_blk):
                row = score[j:j + 1, :]
                for gi, grp in enumerate(groups):
                    if j < 8 * gi:
                        beats = row >= grp
                    elif j >= 8 * gi + 8:
                        beats = row > grp
                    else:
                        beats = (row > grp) | ((row == grp) & (sub8 > j - 8 * gi))
                    ranks[gi] = ranks[gi] + jnp.where(beats, 1.0, 0.0)
            rank = jnp.concatenate(ranks, 0)
            bias = jnp.where(rank < min(N_SEL, n_blk), 0.0, NEG_INF)
            bias = jnp.concatenate([bias, jnp.zeros((LANES - n_blk, tq), F32)], 0)
            sb_ref[0, :, (2 * gp + half) * LANES:(2 * gp + half + 1) * LANES] = bias.T.astype(BF16)


def _cmp_select(q, kc, vct, ovt, gate, tq=128):
    B, S, _ = q.shape
    n_cmp_pad = kc.shape[2]
    n_blk = S // SEL_BLOCK
    tok = lambda width: pl.BlockSpec((1, tq, width), lambda b, i: (b, i, 0))
    return pl.pallas_call(
        functools.partial(_cmp_select_body, (tq, n_cmp_pad, n_blk)),
        grid=(B, S // tq),
        in_specs=[tok(MIX_WIDTH), pl.BlockSpec((1, 2, n_cmp_pad, LANES), lambda b, i: (b, 0, 0, 0)),
                  pl.BlockSpec((1, 2, LANES, n_cmp_pad), lambda b, i: (b, 0, 0, 0)),
                  pl.BlockSpec(ovt.shape, lambda b, i: (0, 0)), tok(2 * LANES)],
        out_specs=[tok(MIX_WIDTH), tok(4 * LANES)],
        out_shape=[jax.ShapeDtypeStruct((B, S, MIX_WIDTH), BF16), jax.ShapeDtypeStruct((B, S, 4 * LANES), BF16)],
        compiler_params=_cparams(("parallel", "parallel")),
        name="nsa_cmp_select",
    )(q, kc, vct, ovt, gate)


def _out_body(n_o, *refs):
    o_refs = refs[:n_o]
    z_ref, x_ref, p_ref, wo_ref, g_ref, b_ref, wg_ref, wp_ref, out_ref = refs[n_o:]
    tm = x_ref.shape[1]
    blocks = [slice(r0, r0 + OUT_ROWS) for r0 in range(0, tm, OUT_ROWS)]
    gated = []
    for rs in blocks:
        o = o_refs[0][0, rs].astype(F32)
        for r in o_refs[1:]:
            o = o + r[0, rs].astype(F32)
        gated.append((o * jax.nn.silu(z_ref[0, rs].astype(F32))).astype(BF16))
    y = [jnp.dot(g, wo_ref[...], preferred_element_type=F32) for g in gated]
    emb = [jnp.dot(p_ref[0, 0, rs].astype(BF16), wp_ref[...], preferred_element_type=F32) for rs in blocks]
    xn = []
    for rs, yb in zip(blocks, y):
        u = DEEPNORM_ALPHA * x_ref[0, rs] + yb
        mu = jnp.mean(u, -1, keepdims=True)
        var = jnp.mean(jnp.square(u - mu), -1, keepdims=True)
        xn.append((u - mu) * lax.rsqrt(var + LN_EPS) * g_ref[...] + b_ref[...])
    gate = [jax.nn.sigmoid(jnp.dot(v.astype(BF16), wg_ref[...], preferred_element_type=F32)) for v in xn]
    for rs, v, gt, e in zip(blocks, xn, gate, emb):
        out_ref[0, rs] = v + gt * e


def _out_block(o_list, z, x, p, layer, wo, g, b, wg, wp, tm=512):
    B, S, _ = x.shape
    tok = lambda width: pl.BlockSpec((1, tm, width), lambda bb, i: (bb, i, 0))
    full = lambda a: pl.BlockSpec(a.shape, lambda bb, i: (0,) * a.ndim)
    n_o = len(o_list)
    return pl.pallas_call(
        functools.partial(_out_body, n_o),
        grid=(B, S // tm),
        in_specs=[tok(MIX_WIDTH)] * n_o + [tok(MIX_WIDTH), tok(D_MODEL),
                                           pl.BlockSpec((1, 1, tm, PLE_DIM), lambda bb, i: (layer, bb, i, 0)),
                                           full(wo), full(g), full(b), full(wg), full(wp)],
        out_specs=tok(D_MODEL),
        out_shape=jax.ShapeDtypeStruct((B, S, D_MODEL), F32),
        compiler_params=_cparams(("parallel", "parallel")),
        name="out_block",
    )(*o_list, z, x, p, wo, g, b, wg, wp)


def _swa_weights(w_in):
    q, k, v, z = jnp.split(w_in, [MIX_WIDTH, MIX_WIDTH + KV_WIDTH, MIX_WIDTH + 2 * KV_WIDTH], axis=1)
    return jnp.concatenate([q[:, _QPERM], z[:, _OPERM], k[:, _KPERM], v], 1).astype(BF16)


def _nsa_weights(w_in):
    cuts = np.cumsum([MIX_WIDTH] + [KV_WIDTH] * 6 + [3 * N_HEADS]).tolist()
    q, k_c, v_c, k_s, v_s, k_w, v_w, gl, z = jnp.split(w_in, cuts, axis=1)
    q = q[:, _QPERM]
    gl = jnp.concatenate(
        [jnp.pad(gl[:, np.array([br * N_HEADS + _POS_HEAD[gp * 8 + c] for br in range(3) for c in range(8)])],
                 ((0, 0), (0, LANES - 24))) for gp in range(2)], 1)
    return jnp.concatenate([q, z[:, _OPERM], k_s[:, _KPERM], v_s, k_w[:, _KPERM], v_w, k_c, v_c, gl], 1).astype(BF16)


def _phi_weights(w1, w2, rope_layout):
    h = w1.shape[1]
    w1 = w1.reshape(2, CMP_STRIDE, HEAD_DIM, h)
    eye = jnp.eye(2, dtype=w1.dtype)
    w1a, w1b = (jnp.einsum("ldj,gh->lgdhj", w1[t], eye).reshape(CMP_STRIDE * 2 * HEAD_DIM, 2 * h) for t in range(2))
    if rope_layout:
        w2 = jnp.einsum("jpi,gh->gjphi", w2.reshape(h, 2, 32), eye).reshape(2 * h, LANES)
    else:
        w2 = jnp.einsum("jd,gh->gjhd", w2, eye).reshape(2 * h, LANES)
    return w1a.astype(BF16), w1b.astype(BF16), w2.astype(BF16)


def _overlap_matrix_t(n_cmp, n_cmp_pad, n_blk):
    cstart = np.arange(n_cmp_pad) * CMP_STRIDE
    j = np.arange(LANES)
    ov = (cstart[None, :] < (j[:, None] + 1) * SEL_BLOCK) & (cstart[None, :] + CMP_BLOCK > j[:, None] * SEL_BLOCK)
    ov &= (np.arange(n_cmp_pad)[None, :] < n_cmp) & (j[:, None] < n_blk)
    return jnp.asarray(ov.astype(np.float32), dtype=BF16)


def _layer_swa(x, tabs64, w_in, sinks):
    q, z, k, vt = _proj_gqa(x, _swa_weights(w_in), *tabs64, n_kv=1, with_cmp=False)
    o = _flash_gqa(q, k, vt, mode="band", window=SWA_WINDOW, sinks=sinks.astype(F32), nsub=4)
    return [o], z, _OPERM


def _layer_mla(x, tabs32, w_in, q_norm, kv_norm, w_uq, w_ukv):
    lat = MLA_Q_LORA + MLA_KV_LORA
    w1 = jnp.concatenate([w_in[:, :lat], w_in[:, lat:lat + MLA_ROPE][:, _KRPERM], w_in[:, lat + MLA_ROPE:]], 1)
    wkv = jnp.concatenate([w_ukv[:, _KNPERM], w_ukv[:, _VPERM]], 1)
    q, k, vt, kr, z = _proj_mla(x, w1.astype(BF16), q_norm[None, :], kv_norm[None, :],
                                w_uq[:, _UQPERM].astype(BF16), wkv.astype(BF16), *tabs32)
    return [_flash_mla(q, k, vt, kr)], z, None


def _layer_nsa(x, tabs64, positions, w_in, cmp_pos, phi_k1, phi_k2, phi_v1, phi_v2):
    B, S, _ = x.shape
    q, z, ks, vst, kw, vwt, kraw, vraw, gate = _proj_gqa(x, _nsa_weights(w_in), *tabs64, n_kv=2, with_cmp=True)
    n_chunk = S // CMP_STRIDE
    n_cmp = (S - CMP_BLOCK) // CMP_STRIDE + 1
    pos2 = jnp.tile(cmp_pos.reshape(2, CMP_STRIDE, 1, HEAD_DIM), (1, 1, 2, 1)).reshape(2, 1, CMP_STRIDE * LANES)
    end = jnp.minimum(jnp.arange(n_chunk) * CMP_STRIDE + CMP_BLOCK - 1, S - 1)
    cos_c, sin_c = _rope_tables(positions[:, end].reshape(-1), HEAD_DIM // 2)
    kc, vct = _compress(kraw, vraw, pos2[0], pos2[1], _phi_weights(phi_k1, phi_k2, True),
                        _phi_weights(phi_v1, phi_v2, False),
                        cos_c.reshape(B, n_chunk, LANES), sin_c.reshape(B, n_chunk, LANES))
    ovt = _overlap_matrix_t(n_cmp, n_chunk, S // SEL_BLOCK)
    o_cmp, sel = _cmp_select(q, kc, vct, ovt, gate)
    o_slc = _flash_gqa(q, ks, vst, mode="causal", sel=sel, gate=gate, gate_base=8, tq=256)
    o_win = _flash_gqa(q, kw, vwt, mode="band", window=NSA_WINDOW, gate=gate, gate_base=16, nsub=2)
    return [o_cmp, o_slc, o_win], z, _OPERM


def kernel(x, p, positions, w_out, ln_g, ln_b, pe_gate, pe_proj, l0_w_in, l0_sinks, l1_w_in, l1_q_norm, l1_kv_norm, l1_w_uq, l1_w_ukv, l2_w_in, l2_cmp_pos, l2_phi_k1, l2_phi_k2, l2_phi_v1, l2_phi_v2, l3_w_in, l3_sinks):
    B, S, _ = x.shape
    pos_flat = positions.reshape(-1)
    tabs64 = [t.reshape(B, S, LANES) for t in _rope_tables(pos_flat, HEAD_DIM // 2)]
    tabs32 = [t.reshape(B, S, LANES) for t in _rope_tables(pos_flat, MLA_ROPE // 2)]
    layers = (
        lambda h: _layer_swa(h, tabs64, l0_w_in, l0_sinks),
        lambda h: _layer_mla(h, tabs32, l1_w_in, l1_q_norm, l1_kv_norm, l1_w_uq, l1_w_ukv),
        lambda h: _layer_nsa(h, tabs64, positions, l2_w_in, l2_cmp_pos, l2_phi_k1, l2_phi_k2, l2_phi_v1, l2_phi_v2),
        lambda h: _layer_swa(h, tabs64, l3_w_in, l3_sinks),
    )
    for i, layer in enumerate(layers):
        o_list, z, operm = layer(x)
        wo = w_out[i] if operm is None else w_out[i][operm, :]
        x = _out_block(o_list, z, x, p, i, wo.astype(BF16), ln_g[i][None, :], ln_b[i][None, :],
                       pe_gate[i].astype(BF16), pe_proj[i].astype(BF16))
    return x
```

```python
import functools

import numpy as np
import jax
import jax.numpy as jnp
from jax import lax
from jax.experimental import pallas as pl
from jax.experimental.pallas import tpu as pltpu

F32 = jnp.float32
BF16 = jnp.bfloat16

LANES = 128
VMEM_LIMIT = 56 * 1024 * 1024

D_MODEL = 1024
DEPTH = 4
HEAD_DIM = 64
N_HEADS = 16
MIX_WIDTH = 1024
ROPE_THETA = 10000.0
NEG_INF = -1e30
PLE_DIM = 256
DEEPNORM_ALPHA = (2 * DEPTH) ** 0.25
LN_EPS = 1e-5
RMS_EPS = 1e-6

SWA_WINDOW = 128
KV_GROUPS = 4
KV_WIDTH = KV_GROUPS * HEAD_DIM

MLA_Q_LORA = 384
MLA_KV_LORA = 256
MLA_NOPE = 64
MLA_ROPE = 32
MLA_V = 64

CMP_BLOCK = 32
CMP_STRIDE = 16
SEL_BLOCK = 64
N_SEL = 8
SEL_FORCE = 1e9
NSA_WINDOW = 512
PHI_HIDDEN = 256

BAND_BLOCK = 128
OUT_ROWS = 128
LOG2E = 1.4426950408889634


def _cparams(sem, flags=None):
    return pltpu.CompilerParams(dimension_semantics=sem, vmem_limit_bytes=VMEM_LIMIT, flags=flags)


_ATTN_FLAGS = None


def _pair_head(gp, r, half):
    return (2 * gp + half) * 4 + r


def _gqa_perms():
    qperm, kperm, operm = [], [], []
    for gp in range(2):
        for r in range(4):
            for part in range(2):
                for half in range(2):
                    h = _pair_head(gp, r, half)
                    qperm += [h * 64 + part * 32 + i for i in range(32)]
            for half in range(2):
                h = _pair_head(gp, r, half)
                operm += [h * 64 + d for d in range(64)]
        for part in range(2):
            for half in range(2):
                g = 2 * gp + half
                kperm += [g * 64 + part * 32 + i for i in range(32)]
    return np.array(qperm), np.array(kperm), np.array(operm)


_QPERM, _KPERM, _OPERM = _gqa_perms()
_POS_HEAD = np.array([_pair_head(gp, r, half) for gp in range(2) for r in range(4) for half in range(2)])


def _mla_perms():
    uq, kr = [], []
    for hg in range(4):
        for a in range(4):
            uq += [(4 * hg + a) * 96 + d for d in range(64)]
        for part in range(2):
            for a in range(4):
                uq += [(4 * hg + a) * 96 + 64 + part * 16 + i for i in range(16)]
    for part in range(2):
        for _ in range(4):
            kr += [part * 16 + i for i in range(16)]
    kn = [h * 128 + d for h in range(16) for d in range(64)]
    vv = [h * 128 + 64 + d for h in range(16) for d in range(64)]
    return np.array(uq), np.array(kr), np.array(kn), np.array(vv)


_UQPERM, _KRPERM, _KNPERM, _VPERM = _mla_perms()


def _rope_lane_tables(half):
    inv = ROPE_THETA ** (-jnp.arange(half, dtype=F32) / half)
    lane = np.arange(LANES)
    inv_l = inv[lane % half][None, :]
    sign = np.where(lane < 64, -1.0, 1.0).astype(np.float32)[None, :]
    return inv_l, jnp.asarray(sign)


def _rope_table_body(pos_ref, inv_ref, sign_ref, cos_ref, sin_ref):
    ang = pos_ref[...].astype(F32) * inv_ref[...]
    cos_ref[...] = jnp.cos(ang)
    sin_ref[...] = jnp.sin(ang) * sign_ref[...]


def _rope_tables(pos, half):
    n = pos.shape[0]
    tr = min(n, 1024)
    inv_l, sign = _rope_lane_tables(half)
    row = pl.BlockSpec((tr, LANES), lambda i: (i, 0))
    cst = pl.BlockSpec((1, LANES), lambda i: (0, 0))
    return pl.pallas_call(
        _rope_table_body,
        grid=(n // tr,),
        in_specs=[pl.BlockSpec((tr, 1), lambda i: (i, 0)), cst, cst],
        out_specs=[row, row],
        out_shape=[jax.ShapeDtypeStruct((n, LANES), F32)] * 2,
        compiler_params=_cparams(("parallel",)),
        name="rope_tables",
    )(pos.reshape(n, 1), inv_l, sign)


def _rope_cols(v, cos, sin):
    return v * cos + pltpu.roll(v, 64, 1) * sin


def _proj_gqa_body(n_kv, with_cmp, x_ref, w_ref, cos_ref, sin_ref, *outs):
    xb = x_ref[0].astype(BF16)
    cos = cos_ref[0]
    sin = sin_ref[0]

    def mm(c0, width):
        return jnp.dot(xb, w_ref[:, c0:c0 + width], preferred_element_type=F32)

    q_ref, z_ref = outs[0], outs[1]
    for c in range(0, MIX_WIDTH, 512):
        acc = mm(c, 512) * (HEAD_DIM ** -0.5 * LOG2E)
        for j in range(0, 512, LANES):
            q_ref[0, :, c + j:c + j + LANES] = _rope_cols(acc[:, j:j + LANES], cos, sin).astype(BF16)
    for c in range(0, MIX_WIDTH, 512):
        z_ref[0, :, c:c + 512] = mm(MIX_WIDTH + c, 512).astype(BF16)
    col = 2 * MIX_WIDTH
    o = 2
    for _ in range(n_kv):
        acc = mm(col, 2 * KV_WIDTH)
        for j in range(0, KV_WIDTH, LANES):
            outs[o][0, :, j:j + LANES] = _rope_cols(acc[:, j:j + LANES], cos, sin).astype(BF16)
        outs[o + 1][0] = acc[:, KV_WIDTH:].T.astype(BF16)
        col += 2 * KV_WIDTH
        o += 2
    if with_cmp:
        acc = mm(col, 2 * KV_WIDTH)
        for t in range(2):
            for gp in range(2):
                c0 = t * KV_WIDTH + gp * LANES
                outs[o + t][0, gp] = acc[:, c0:c0 + LANES]
        col += 2 * KV_WIDTH
        outs[o + 2][0] = jax.nn.sigmoid(mm(col, 2 * LANES))


def _proj_gqa(x, w, cos, sin, n_kv, with_cmp, tm=512):
    B, S, _ = x.shape
    n = w.shape[1]
    tok = lambda width: pl.BlockSpec((1, tm, width), lambda b, i: (b, i, 0))
    out_specs = [tok(MIX_WIDTH), tok(MIX_WIDTH)]
    out_shape = [jax.ShapeDtypeStruct((B, S, MIX_WIDTH), BF16)] * 2
    for _ in range(n_kv):
        out_specs += [tok(KV_WIDTH), pl.BlockSpec((1, KV_WIDTH, tm), lambda b, i: (b, 0, i))]
        out_shape += [jax.ShapeDtypeStruct((B, S, KV_WIDTH), BF16), jax.ShapeDtypeStruct((B, KV_WIDTH, S), BF16)]
    if with_cmp:
        raw = pl.BlockSpec((1, 2, tm, LANES), lambda b, i: (b, 0, i, 0))
        out_specs += [raw, raw, tok(2 * LANES)]
        out_shape += [jax.ShapeDtypeStruct((B, 2, S, LANES), F32)] * 2
        out_shape += [jax.ShapeDtypeStruct((B, S, 2 * LANES), F32)]
    return pl.pallas_call(
        functools.partial(_proj_gqa_body, n_kv, with_cmp),
        grid=(B, S // tm),
        in_specs=[tok(D_MODEL), pl.BlockSpec((D_MODEL, n), lambda b, i: (0, 0)), tok(LANES), tok(LANES)],
        out_specs=out_specs,
        out_shape=out_shape,
        compiler_params=_cparams(("parallel", "parallel")),
        name="proj_gqa",
    )(x, w, cos, sin)


def _rms(v, g):
    return v * lax.rsqrt(jnp.mean(v * v, -1, keepdims=True) + RMS_EPS) * g


def _proj_mla_body(x_ref, w1_ref, qn_ref, kvn_ref, wuq_ref, wkv_ref, cos_ref, sin_ref,
                   q_ref, k_ref, v_ref, kr_ref, z_ref):
    xb = x_ref[0].astype(BF16)
    cos = cos_ref[0]
    sin = sin_ref[0]
    lat = MLA_Q_LORA + MLA_KV_LORA
    c = jnp.dot(xb, w1_ref[:, :lat + LANES], preferred_element_type=F32)
    cq = _rms(c[:, :MLA_Q_LORA], qn_ref[...]).astype(BF16)
    ckv = _rms(c[:, MLA_Q_LORA:lat], kvn_ref[...]).astype(BF16)
    kr_ref[0] = _rope_cols(c[:, lat:lat + LANES], cos, sin).astype(BF16)
    scale = (MLA_NOPE + MLA_ROPE) ** -0.5 * LOG2E
    for hg in range(4):
        c0 = hg * 384
        q = jnp.dot(cq, wuq_ref[:, c0:c0 + 384], preferred_element_type=F32) * scale
        q_ref[0, :, c0:c0 + 256] = q[:, :256].astype(BF16)
        q_ref[0, :, c0 + 256:c0 + 384] = _rope_cols(q[:, 256:], cos, sin).astype(BF16)
    for c0 in range(0, MIX_WIDTH, 512):
        k_ref[0, :, c0:c0 + 512] = jnp.dot(ckv, wkv_ref[:, c0:c0 + 512], preferred_element_type=F32).astype(BF16)
        v_ref[0, c0:c0 + 512, :] = jnp.dot(ckv, wkv_ref[:, MIX_WIDTH + c0:MIX_WIDTH + c0 + 512],
                                            preferred_element_type=F32).T.astype(BF16)
        z_ref[0, :, c0:c0 + 512] = jnp.dot(xb, w1_ref[:, lat + LANES + c0:lat + LANES + c0 + 512],
                                            preferred_element_type=F32).astype(BF16)


def _proj_mla(x, w1, qn, kvn, wuq, wkv, cos, sin, tm=512):
    B, S, _ = x.shape
    tok = lambda width: pl.BlockSpec((1, tm, width), lambda b, i: (b, i, 0))
    full = lambda a: pl.BlockSpec(a.shape, lambda b, i: (0,) * a.ndim)
    widths = [16 * 96, MIX_WIDTH, MIX_WIDTH, LANES, MIX_WIDTH]
    out_specs = [tok(w) for w in widths]
    out_shape = [jax.ShapeDtypeStruct((B, S, w), BF16) for w in widths]
    out_specs[2] = pl.BlockSpec((1, MIX_WIDTH, tm), lambda b, i: (b, 0, i))
    out_shape[2] = jax.ShapeDtypeStruct((B, MIX_WIDTH, S), BF16)
    return pl.pallas_call(
        _proj_mla_body,
        grid=(B, S // tm),
        in_specs=[tok(D_MODEL), full(w1), full(qn), full(kvn), full(wuq), full(wkv), tok(LANES), tok(LANES)],
        out_specs=out_specs,
        out_shape=out_shape,
        compiler_params=_cparams(("parallel", "parallel")),
        name="proj_mla",
    )(x, w1, qn, kvn, wuq, wkv, cos, sin)


_NT = (((1,), (1,)), ((), ()))


def _lane_iota():
    return lax.broadcasted_iota(jnp.int32, (1, LANES), 1)


def _first_half_rows():
    return lax.broadcasted_iota(jnp.int32, (LANES, 1), 0) < 64


def _pair_lhs(q_cols, extra_a=None, extra_b=None):
    m_a = ((_lane_iota() >> 5) & 1) == 0
    a_rows, b_rows = [], []
    for qc in q_cols:
        a = jnp.where(m_a, qc, 0)
        b = jnp.where(m_a, 0, qc)
        if extra_a is not None:
            a = jnp.concatenate([a, extra_a], 1)
            b = jnp.concatenate([b, extra_b], 1)
        a_rows.append(a)
        b_rows.append(b)
    return jnp.concatenate(a_rows + b_rows, 0)


def _tile_lanes(bias, reps):
    return jnp.concatenate([bias] * reps, 1)


def _online_step(lhs, kt, vt, bias, state):
    return _softmax_pv(_scores(lhs, kt), vt, bias, state)


def _scores(lhs, kt):
    return lax.dot_general(kt, lhs, _NT, preferred_element_type=F32)


def _softmax_pv(s, vt, bias, state):
    m, acc_a, acc_b = state
    half = s.shape[1] // 2
    if bias is not None:
        s = s + bias
    m_new = jnp.maximum(m, jnp.max(s, 0, keepdims=True))
    p = jnp.exp2(s - m_new).astype(BF16)
    top = _first_half_rows()
    pv_a = jnp.dot(jnp.where(top, vt, 1), p[:, :half], preferred_element_type=F32)
    pv_b = jnp.dot(jnp.where(top, 1, vt), p[:, half:], preferred_element_type=F32)
    if acc_a is None:
        return m_new, pv_a, pv_b
    alpha = jnp.exp2(m - m_new)
    return m_new, acc_a * alpha[:, :half] + pv_a, acc_b * alpha[:, half:] + pv_b


def _causal_sweep(score_fn, pv_fn, n_chain, n_full, tk, bias, init, buf_a, buf_b):
    def scores_to(buf, start, bias=None):
        start = pl.multiple_of(start, BAND_BLOCK)
        for c in range(n_chain):
            s = score_fn(c, start)
            buf[c] = s if bias is None else s + bias

    def consume(buf, start, states):
        start = pl.multiple_of(start, BAND_BLOCK)
        return tuple(pv_fn(c, buf[c], start, states[c]) for c in range(n_chain))

    diag_start = n_full * tk
    scores_to(buf_a, diag_start, bias)

    def two_tiles(i, carry):
        states, prev = carry
        scores_to(buf_b, 2 * i * tk)
        states = consume(buf_a, prev, states)
        scores_to(buf_a, (2 * i + 1) * tk)
        states = consume(buf_b, 2 * i * tk, states)
        return states, (2 * i + 1) * tk

    states, prev = lax.fori_loop(0, n_full // 2, two_tiles, (init, diag_start))

    def odd_tail(states):
        last = (n_full - 1) * tk
        scores_to(buf_b, last)
        return consume(buf_b, last, consume(buf_a, prev, states))

    return lax.cond(n_full % 2 == 1, odd_tail, lambda st: consume(buf_a, prev, st), states)


def _finish(state, gate_rows):
    _, acc_a, acc_b = state
    half = acc_a.shape[1]
    inv_a = 1.0 / acc_a[64:65, :]
    inv_b = 1.0 / acc_b[0:1, :]
    if gate_rows is not None:
        inv_a = inv_a * gate_rows[:, :half]
        inv_b = inv_b * gate_rows[:, half:]
    return jnp.where(_first_half_rows(), acc_a * inv_a, acc_b * inv_b)


def _gate_rows(gate_t, cols):
    return jnp.concatenate([gate_t[c:c + 1, :] for c in cols], 1)


def _flash_gqa_body(cfg, *refs):
    tq, nsub, n_prev, mode, use_sel, use_sink, gate_base, tk = cfg
    refs = list(refs)
    q_ref, k_ref, vt_ref, bias_ref = refs[:4]
    nxt = 4
    sb_ref = gate_ref = sink_ref = None
    if use_sel:
        sb_ref = refs[nxt]; nxt += 1
    if gate_base is not None:
        gate_ref = refs[nxt]; nxt += 1
    if use_sink:
        sink_ref = refs[nxt]; nxt += 1
    o_ref = refs[nxt]

    step_i = pl.program_id(1)
    rows = 8 * tq
    top = _first_half_rows()
    chains = [(sub, gp) for sub in range(nsub) for gp in range(2)]
    lhs, init = [], []
    for sub, gp in chains:
        tok = slice(sub * tq, (sub + 1) * tq)
        q_cols = [q_ref[0, tok, (4 * gp + r) * LANES:(4 * gp + r + 1) * LANES] for r in range(4)]
        if use_sel:
            lhs.append(_pair_lhs(q_cols, sb_ref[0, tok, 2 * gp * LANES:(2 * gp + 1) * LANES],
                                 sb_ref[0, tok, (2 * gp + 1) * LANES:(2 * gp + 2) * LANES]))
        else:
            lhs.append(_pair_lhs(q_cols))
        if use_sink:
            m0 = jnp.concatenate([jnp.full((1, tq), sink_ref[_pair_head(gp, r, half)] * LOG2E, F32)
                                  for half in range(2) for r in range(4)], 1)
            init.append((m0, jnp.where(top, jnp.zeros((LANES, rows // 2), F32), 1.0),
                         jnp.where(top, jnp.ones((LANES, rows // 2), F32), 0.0)))
        else:
            init.append((jnp.full((1, rows), NEG_INF, F32), None, None))

    def step(starts, states, biases):
        starts = [pl.multiple_of(st, BAND_BLOCK) for st in starts]
        s = []
        for c, (sub, gp) in enumerate(chains):
            kt = k_ref[0, pl.ds(starts[c], tk), gp * LANES:(gp + 1) * LANES]
            if use_sel:
                blk = (starts[c] + lax.broadcasted_iota(jnp.int32, (tk, LANES), 0)) >> 6
                onehot = (blk == lax.broadcasted_iota(jnp.int32, (tk, LANES), 1)).astype(BF16)
                kt = jnp.concatenate([kt, onehot], 1)
            s.append(_scores(lhs[c], kt))
        if mode == "band":
            buf = refs[nxt + 1]
            for c in range(len(chains)):
                buf[c] = s[c]
            s = [buf[c] for c in range(len(chains))]
        return tuple(
            _softmax_pv(s[c], vt_ref[0, gp * LANES:(gp + 1) * LANES, pl.ds(starts[c], tk)], biases[c], states[c])
            for c, (sub, gp) in enumerate(chains))

    if mode == "band":
        tiles = [step_i * nsub + sub for sub, gp in chains]
        states = step([jnp.maximum(t * tq - (tk - tq), 0) for t in tiles], init,
                      [_tile_lanes(bias_ref[jnp.minimum(t, n_prev)], 8) for t in tiles])
    else:
        buf_a, buf_b = refs[nxt + 1], refs[nxt + 2]

        def score_fn(c, start):
            gp = chains[c][1]
            kt = k_ref[0, pl.ds(start, tk), gp * LANES:(gp + 1) * LANES]
            if use_sel:
                blk = (start + lax.broadcasted_iota(jnp.int32, (tk, LANES), 0)) >> 6
                onehot = (blk == lax.broadcasted_iota(jnp.int32, (tk, LANES), 1)).astype(BF16)
                kt = jnp.concatenate([kt, onehot], 1)
            return _scores(lhs[c], kt)

        def pv_fn(c, s, start, state):
            gp = chains[c][1]
            return _softmax_pv(s, vt_ref[0, gp * LANES:(gp + 1) * LANES, pl.ds(start, tk)], None, state)

        zeros = jnp.zeros((LANES, rows // 2), F32)
        states = _causal_sweep(score_fn, pv_fn, len(chains), (step_i * tq) // tk, tk,
                               _tile_lanes(bias_ref[step_i % (tk // tq)], 8),
                               tuple((m0, zeros, zeros) for m0, _, _ in init), buf_a, buf_b)

    for c, (sub, gp) in enumerate(chains):
        tok = slice(sub * tq, (sub + 1) * tq)
        gate_rows = None
        if gate_base is not None:
            gate_t = gate_ref[0, tok, gp * LANES:(gp + 1) * LANES].T
            gate_rows = _gate_rows(gate_t, [gate_base + 2 * r for r in range(4)]
                                   + [gate_base + 2 * r + 1 for r in range(4)])
        out_t = _finish(states[c], gate_rows)
        for r in range(4):
            o_ref[0, tok, (4 * gp + r) * LANES:(4 * gp + r + 1) * LANES] = (
                out_t[:, r * tq:(r + 1) * tq].T.astype(o_ref.dtype))


def _band_bias(window, tq, tk):
    pad = tk - tq
    r = np.arange(tk)[:, None]
    c = np.arange(tq)[None, :]
    out = []
    for v in range(pad // tq + 1):
        d = r - c - (v * tq if v < pad // tq else pad)
        out.append(np.where((d <= 0) & (d > -window), 0.0, NEG_INF))
    return jnp.asarray(np.stack(out), F32)


def _diag_bias(tq, tk):
    r = np.arange(tk)[:, None]
    c = np.arange(tq)[None, :]
    return jnp.asarray(np.stack([np.where(r <= v * tq + c, 0.0, NEG_INF) for v in range(tk // tq)]), F32)


def _flash_gqa(q, k, vt, *, mode, window=None, sel=None, gate=None, gate_base=None, sinks=None,
               tq=128, nsub=1, tk=512):
    B, S, _ = q.shape
    n_prev = 0
    if mode == "band":
        pad = -(-(window - 1) // BAND_BLOCK) * BAND_BLOCK
        tk = pad + tq
        bias = _band_bias(window, tq, tk)
        n_prev = pad // tq
    else:
        assert nsub == 1
        bias = _diag_bias(tq, tk)
    bias_spec = pl.BlockSpec(bias.shape, lambda b, i: (0, 0, 0))
    cfg = (tq, nsub, n_prev, mode, sel is not None, sinks is not None, gate_base if gate is not None else None, tk)
    tok = lambda width: pl.BlockSpec((1, nsub * tq, width), lambda b, i: (b, i, 0))
    in_specs = [tok(MIX_WIDTH), pl.BlockSpec((1, S, KV_WIDTH), lambda b, i: (b, 0, 0)),
                pl.BlockSpec((1, KV_WIDTH, S), lambda b, i: (b, 0, 0)), bias_spec]
    args = [q, k, vt, bias]
    if sel is not None:
        in_specs.append(tok(4 * LANES))
        args.append(sel)
    if gate is not None:
        in_specs.append(tok(2 * LANES))
        args.append(gate)
    if sinks is not None:
        in_specs.append(pl.BlockSpec(memory_space=pltpu.SMEM))
        args.append(sinks)
    return pl.pallas_call(
        functools.partial(_flash_gqa_body, cfg),
        grid=(B, S // (nsub * tq)),
        in_specs=in_specs,
        out_specs=tok(MIX_WIDTH),
        out_shape=jax.ShapeDtypeStruct((B, S, MIX_WIDTH), BF16),
        scratch_shapes=([pltpu.VMEM((2 * nsub, tk, 8 * tq), F32)] if mode == "band"
                        else [pltpu.VMEM((2, tk, 8 * tq), F32)] * 2),
        compiler_params=_cparams(("parallel", "arbitrary"), _ATTN_FLAGS),
        name=f"attn_{mode}{'_sel' if sel is not None else ''}{window or ''}",
    )(*args)


def _flash_mla_body(cfg, q_ref, k_ref, vt_ref, kr_ref, bias_ref, o_ref, buf_a, buf_b):
    tq, nhg, tk = cfg
    step_i = pl.program_id(2)
    q0 = step_i * tq
    lane = _lane_iota()
    lo = lane < 64
    slot = (lane >> 4) & 3
    n_pair = 2 * nhg
    lhs = []
    for hg in range(nhg):
        qr = q_ref[0, :, hg * 384 + 256:hg * 384 + 384]
        for pair in range(2):
            qn = q_ref[0, :, hg * 384 + pair * LANES:hg * 384 + (pair + 1) * LANES]
            a = jnp.concatenate([jnp.where(lo, qn, 0), jnp.where(slot == 2 * pair, qr, 0)], 1)
            b = jnp.concatenate([jnp.where(lo, 0, qn), jnp.where(slot == 2 * pair + 1, qr, 0)], 1)
            lhs.append(jnp.concatenate([a, b], 0))
    rows = 2 * tq
    init = (jnp.full((1, rows), NEG_INF, F32), jnp.zeros((LANES, tq), F32), jnp.zeros((LANES, tq), F32))

    def score_fn(c, start):
        kt = jnp.concatenate([k_ref[0, pl.ds(start, tk), c * LANES:(c + 1) * LANES],
                              kr_ref[0, pl.ds(start, tk), :]], 1)
        return _scores(lhs[c], kt)

    def pv_fn(c, s, start, state):
        return _softmax_pv(s, vt_ref[0, c * LANES:(c + 1) * LANES, pl.ds(start, tk)], None, state)

    states = _causal_sweep(score_fn, pv_fn, n_pair, q0 // tk, tk, _tile_lanes(bias_ref[step_i % (tk // tq)], 2),
                           (init,) * n_pair, buf_a, buf_b)
    for c in range(n_pair):
        o_ref[0, :, c * LANES:(c + 1) * LANES] = _finish(states[c], None).T.astype(o_ref.dtype)


def _flash_mla(q, k, vt, kr, tq=512, tk=512, nhg=2):
    B, S, _ = q.shape
    bias = _diag_bias(tq, tk)
    return pl.pallas_call(
        functools.partial(_flash_mla_body, (tq, nhg, tk)),
        grid=(B, 4 // nhg, S // tq),
        in_specs=[pl.BlockSpec((1, tq, nhg * 384), lambda b, g, i: (b, i, g)),
                  pl.BlockSpec((1, S, nhg * 256), lambda b, g, i: (b, 0, g)),
                  pl.BlockSpec((1, nhg * 256, S), lambda b, g, i: (b, g, 0)),
                  pl.BlockSpec((1, S, LANES), lambda b, g, i: (b, 0, 0)),
                  pl.BlockSpec(bias.shape, lambda b, g, i: (0, 0, 0))],
        out_specs=pl.BlockSpec((1, tq, nhg * 256), lambda b, g, i: (b, i, g)),
        out_shape=jax.ShapeDtypeStruct((B, S, MIX_WIDTH), BF16),
        scratch_shapes=[pltpu.VMEM((2 * nhg, tk, 2 * tq), F32)] * 2,
        compiler_params=_cparams(("parallel", "parallel", "arbitrary"), _ATTN_FLAGS),
        name="attn_mla",
    )(q, k, vt, kr, bias)


def _compress_body(kraw_ref, vraw_ref, posa_ref, posb_ref, wk1a_ref, wk1b_ref, wk2_ref,
                   wv1a_ref, wv1b_ref, wv2_ref, cos_ref, sin_ref, kc_ref, vct_ref):
    n = kraw_ref.shape[2] // CMP_STRIDE

    def chunks(ref):
        return jnp.concatenate([ref[0, 0, pl.ds(t, n, stride=CMP_STRIDE), :] for t in range(CMP_STRIDE)], 1)

    def phi(raw, w1a, w1b, w2):
        a = jnp.dot((raw + posa_ref[...]).astype(BF16), w1a[...], preferred_element_type=F32)
        b = jnp.dot((raw + posb_ref[...]).astype(BF16), w1b[...], preferred_element_type=F32)
        h = a + pltpu.roll(b, n - 1, 0)
        return jnp.dot(jax.nn.silu(h).astype(BF16), w2[...], preferred_element_type=F32)

    kc = phi(chunks(kraw_ref), wk1a_ref, wk1b_ref, wk2_ref)
    kc_ref[0, 0] = _rope_cols(kc, cos_ref[0], sin_ref[0]).astype(BF16)
    vct_ref[0, 0] = phi(chunks(vraw_ref), wv1a_ref, wv1b_ref, wv2_ref).T.astype(BF16)


def _compress(kraw, vraw, posa, posb, wk, wv, cos_c, sin_c):
    B, _, S, _ = kraw.shape
    n = S // CMP_STRIDE
    raw = pl.BlockSpec((1, 1, S, LANES), lambda b, g: (b, g, 0, 0))
    full = lambda a: pl.BlockSpec(a.shape, lambda b, g: (0,) * a.ndim)
    tab = pl.BlockSpec((1, n, LANES), lambda b, g: (b, 0, 0))
    return pl.pallas_call(
        _compress_body,
        grid=(B, 2),
        in_specs=[raw, raw, full(posa), full(posb)] + [full(w) for w in wk] + [full(w) for w in wv] + [tab, tab],
        out_specs=[pl.BlockSpec((1, 1, n, LANES), lambda b, g: (b, g, 0, 0)),
                   pl.BlockSpec((1, 1, LANES, n), lambda b, g: (b, g, 0, 0))],
        out_shape=[jax.ShapeDtypeStruct((B, 2, n, LANES), BF16), jax.ShapeDtypeStruct((B, 2, LANES, n), BF16)],
        compiler_params=_cparams(("parallel", "parallel")),
        name="nsa_compress",
    )(kraw, vraw, posa, posb, *wk, *wv, cos_c, sin_c)


def _split3(p):
    hi = p.astype(BF16)
    r1 = p - hi.astype(F32)
    mid = r1.astype(BF16)
    lo = (r1 - mid.astype(F32)).astype(BF16)
    return hi, mid, lo


def _cmp_select_body(cfg, q_ref, kc_ref, vct_ref, ovt_ref, gate_ref, o_ref, sb_ref, s_buf):
    tq, n_cmp_pad, n_blk = cfg
    q0 = pl.program_id(1) * tq
    rows = 8 * tq
    half_rows = rows // 2
    tpos = q0 + (lax.broadcasted_iota(jnp.int32, (1, rows), 1) & (tq - 1))
    blk_end = lax.broadcasted_iota(jnp.int32, (n_cmp_pad, 1), 0) * CMP_STRIDE + (CMP_BLOCK - 1)
    cbias = _tile_lanes(jnp.where(blk_end <= tpos[:, :tq], 0.0, NEG_INF), 8)
    sees_any = tpos >= CMP_BLOCK - 1
    top = _first_half_rows()
    jrow = lax.broadcasted_iota(jnp.int32, (n_blk, 1), 0)
    sub8 = lax.broadcasted_iota(jnp.int32, (8, 1), 0)
    cur = (q0 + lax.broadcasted_iota(jnp.int32, (1, tq), 1)) >> 6
    forced = (jrow == 0) | (jrow == cur) | (jrow == cur - 1)
    ovt = ovt_ref[...]

    for gp in range(2):
        lhs = _pair_lhs([q_ref[0, :, (4 * gp + r) * LANES:(4 * gp + r + 1) * LANES] for r in range(4)])
        s_buf[gp] = lax.dot_general(kc_ref[0, gp], lhs, _NT, preferred_element_type=F32)

    for gp in range(2):
        s = s_buf[gp] + cbias
        m = jnp.max(s, 0, keepdims=True)
        p = jnp.exp2(s - m)
        l = jnp.sum(p, 0, keepdims=True)
        pc = p * jnp.where(sees_any, 1.0 / l, 0.0)

        vct = vct_ref[0, gp]
        pb = pc.astype(BF16)
        pv = (jnp.dot(jnp.where(top, vct, 0), pb[:, :half_rows], preferred_element_type=F32)
              + jnp.dot(jnp.where(top, 0, vct), pb[:, half_rows:], preferred_element_type=F32))
        gate_t = gate_ref[0, :, gp * LANES:(gp + 1) * LANES].T
        ga = _gate_rows(gate_t, [2 * r for r in range(4)])
        gb = _gate_rows(gate_t, [2 * r + 1 for r in range(4)])
        out_t = pv * jnp.where(top, ga, gb)
        for r in range(4):
            o_ref[0, :, (4 * gp + r) * LANES:(4 * gp + r + 1) * LANES] = (
                out_t[:, r * tq:(r + 1) * tq].T.astype(o_ref.dtype))

        for half in range(2):
            c0 = 4 * half * tq
            psum = pc[:, c0:c0 + tq] + pc[:, c0 + tq:c0 + 2 * tq] + pc[:, c0 + 2 * tq:c0 + 3 * tq] + pc[:, c0 + 3 * tq:c0 + 4 * tq]
            imp = sum(jnp.dot(ovt, part, preferred_element_type=F32) for part in _split3(psum))[:n_blk]
            score = jnp.where(forced, SEL_FORCE, jnp.where(jrow <= cur, imp, -SEL_FORCE))
            groups = [score[g:g + 8] for g in range(0, n_blk, 8)]
            ranks = [jnp.zeros((8, tq), F32) for _ in groups]
            for j in range(n_blk):
                row = score[j:j + 1, :]
                for gi, grp in enumerate(groups):
                    if j < 8 * gi:
                        beats = row >= grp
                    elif j >= 8 * gi + 8:
                        beats = row > grp
                    else:
                        beats = (row > grp) | ((row == grp) & (sub8 > j - 8 * gi))
                    ranks[gi] = ranks[gi] + jnp.where(beats, 1.0, 0.0)
            rank = jnp.concatenate(ranks, 0)
            bias = jnp.where(rank < min(N_SEL, n_blk), 0.0, NEG_INF)
            bias = jnp.concatenate([bias, jnp.zeros((LANES - n_blk, tq), F32)], 0)
            sb_ref[0, :, (2 * gp + half) * LANES:(2 * gp + half + 1) * LANES] = bias.T.astype(BF16)


def _cmp_select(q, kc, vct, ovt, gate, tq=128):
    B, S, _ = q.shape
    n_cmp_pad = kc.shape[2]
    n_blk = S // SEL_BLOCK
    tok = lambda width: pl.BlockSpec((1, tq, width), lambda b, i: (b, i, 0))
    return pl.pallas_call(
        functools.partial(_cmp_select_body, (tq, n_cmp_pad, n_blk)),
        grid=(B, S // tq),
        in_specs=[tok(MIX_WIDTH), pl.BlockSpec((1, 2, n_cmp_pad, LANES), lambda b, i: (b, 0, 0, 0)),
                  pl.BlockSpec((1, 2, LANES, n_cmp_pad), lambda b, i: (b, 0, 0, 0)),
                  pl.BlockSpec(ovt.shape, lambda b, i: (0, 0)), tok(2 * LANES)],
        out_specs=[tok(MIX_WIDTH), tok(4 * LANES)],
        out_shape=[jax.ShapeDtypeStruct((B, S, MIX_WIDTH), BF16), jax.ShapeDtypeStruct((B, S, 4 * LANES), BF16)],
        scratch_shapes=[pltpu.VMEM((2, n_cmp_pad, 8 * tq), F32)],
        compiler_params=_cparams(("parallel", "parallel")),
        name="nsa_cmp_select",
    )(q, kc, vct, ovt, gate)


def _out_body(n_o, *refs):
    o_refs = refs[:n_o]
    z_ref, x_ref, p_ref, wo_ref, g_ref, b_ref, wg_ref, wp_ref, out_ref = refs[n_o:]
    tm = x_ref.shape[1]
    blocks = [slice(r0, r0 + OUT_ROWS) for r0 in range(0, tm, OUT_ROWS)]
    gated = []
    for rs in blocks:
        o = o_refs[0][0, rs].astype(F32)
        for r in o_refs[1:]:
            o = o + r[0, rs].astype(F32)
        gated.append((o * jax.nn.silu(z_ref[0, rs].astype(F32))).astype(BF16))
    y = [jnp.dot(g, wo_ref[...], preferred_element_type=F32) for g in gated]
    emb = [jnp.dot(p_ref[0, 0, rs].astype(BF16), wp_ref[...], preferred_element_type=F32) for rs in blocks]
    xn = []
    for rs, yb in zip(blocks, y):
        u = DEEPNORM_ALPHA * x_ref[0, rs] + yb
        mu = jnp.mean(u, -1, keepdims=True)
        var = jnp.mean(jnp.square(u - mu), -1, keepdims=True)
        xn.append((u - mu) * lax.rsqrt(var + LN_EPS) * g_ref[...] + b_ref[...])
    gate = [jax.nn.sigmoid(jnp.dot(v.astype(BF16), wg_ref[...], preferred_element_type=F32)) for v in xn]
    for rs, v, gt, e in zip(blocks, xn, gate, emb):
        out_ref[0, rs] = v + gt * e


def _out_block(o_list, z, x, p, layer, wo, g, b, wg, wp, tm=512):
    B, S, _ = x.shape
    tok = lambda width: pl.BlockSpec((1, tm, width), lambda bb, i: (bb, i, 0))
    full = lambda a: pl.BlockSpec(a.shape, lambda bb, i: (0,) * a.ndim)
    n_o = len(o_list)
    return pl.pallas_call(
        functools.partial(_out_body, n_o),
        grid=(B, S // tm),
        in_specs=[tok(MIX_WIDTH)] * n_o + [tok(MIX_WIDTH), tok(D_MODEL),
                                           pl.BlockSpec((1, 1, tm, PLE_DIM), lambda bb, i: (layer, bb, i, 0)),
                                           full(wo), full(g), full(b), full(wg), full(wp)],
        out_specs=tok(D_MODEL),
        out_shape=jax.ShapeDtypeStruct((B, S, D_MODEL), F32),
        compiler_params=_cparams(("parallel", "parallel")),
        name="out_block",
    )(*o_list, z, x, p, wo, g, b, wg, wp)


def _swa_weights(w_in):
    q, k, v, z = jnp.split(w_in, [MIX_WIDTH, MIX_WIDTH + KV_WIDTH, MIX_WIDTH + 2 * KV_WIDTH], axis=1)
    return jnp.concatenate([q[:, _QPERM], z[:, _OPERM], k[:, _KPERM], v], 1).astype(BF16)


def _nsa_weights(w_in):
    cuts = np.cumsum([MIX_WIDTH] + [KV_WIDTH] * 6 + [3 * N_HEADS]).tolist()
    q, k_c, v_c, k_s, v_s, k_w, v_w, gl, z = jnp.split(w_in, cuts, axis=1)
    q = q[:, _QPERM]
    gl = jnp.concatenate(
        [jnp.pad(gl[:, np.array([br * N_HEADS + _POS_HEAD[gp * 8 + c] for br in range(3) for c in range(8)])],
                 ((0, 0), (0, LANES - 24))) for gp in range(2)], 1)
    return jnp.concatenate([q, z[:, _OPERM], k_s[:, _KPERM], v_s, k_w[:, _KPERM], v_w, k_c, v_c, gl], 1).astype(BF16)


def _phi_weights(w1, w2, rope_layout):
    h = w1.shape[1]
    w1 = w1.reshape(2, CMP_STRIDE, HEAD_DIM, h)
    eye = jnp.eye(2, dtype=w1.dtype)
    w1a, w1b = (jnp.einsum("ldj,gh->lgdhj", w1[t], eye).reshape(CMP_STRIDE * 2 * HEAD_DIM, 2 * h) for t in range(2))
    if rope_layout:
        w2 = jnp.einsum("jpi,gh->gjphi", w2.reshape(h, 2, 32), eye).reshape(2 * h, LANES)
    else:
        w2 = jnp.einsum("jd,gh->gjhd", w2, eye).reshape(2 * h, LANES)
    return w1a.astype(BF16), w1b.astype(BF16), w2.astype(BF16)


def _overlap_matrix_t(n_cmp, n_cmp_pad, n_blk):
    cstart = np.arange(n_cmp_pad) * CMP_STRIDE
    j = np.arange(LANES)
    ov = (cstart[None, :] < (j[:, None] + 1) * SEL_BLOCK) & (cstart[None, :] + CMP_BLOCK > j[:, None] * SEL_BLOCK)
    ov &= (np.arange(n_cmp_pad)[None, :] < n_cmp) & (j[:, None] < n_blk)
    return jnp.asarray(ov.astype(np.float32), dtype=BF16)


def _layer_swa(x, tabs64, w_in, sinks):
    q, z, k, vt = _proj_gqa(x, _swa_weights(w_in), *tabs64, n_kv=1, with_cmp=False)
    o = _flash_gqa(q, k, vt, mode="band", window=SWA_WINDOW, sinks=sinks.astype(F32), nsub=8)
    return [o], z, _OPERM


def _layer_mla(x, tabs32, w_in, q_norm, kv_norm, w_uq, w_ukv):
    lat = MLA_Q_LORA + MLA_KV_LORA
    w1 = jnp.concatenate([w_in[:, :lat], w_in[:, lat:lat + MLA_ROPE][:, _KRPERM], w_in[:, lat + MLA_ROPE:]], 1)
    wkv = jnp.concatenate([w_ukv[:, _KNPERM], w_ukv[:, _VPERM]], 1)
    q, k, vt, kr, z = _proj_mla(x, w1.astype(BF16), q_norm[None, :], kv_norm[None, :],
                                w_uq[:, _UQPERM].astype(BF16), wkv.astype(BF16), *tabs32)
    return [_flash_mla(q, k, vt, kr)], z, None


def _layer_nsa(x, tabs64, positions, w_in, cmp_pos, phi_k1, phi_k2, phi_v1, phi_v2):
    B, S, _ = x.shape
    q, z, ks, vst, kw, vwt, kraw, vraw, gate = _proj_gqa(x, _nsa_weights(w_in), *tabs64, n_kv=2, with_cmp=True)
    n_chunk = S // CMP_STRIDE
    n_cmp = (S - CMP_BLOCK) // CMP_STRIDE + 1
    pos2 = jnp.tile(cmp_pos.reshape(2, CMP_STRIDE, 1, HEAD_DIM), (1, 1, 2, 1)).reshape(2, 1, CMP_STRIDE * LANES)
    end = jnp.minimum(jnp.arange(n_chunk) * CMP_STRIDE + CMP_BLOCK - 1, S - 1)
    cos_c, sin_c = _rope_tables(positions[:, end].reshape(-1), HEAD_DIM // 2)
    kc, vct = _compress(kraw, vraw, pos2[0], pos2[1], _phi_weights(phi_k1, phi_k2, True),
                        _phi_weights(phi_v1, phi_v2, False),
                        cos_c.reshape(B, n_chunk, LANES), sin_c.reshape(B, n_chunk, LANES))
    ovt = _overlap_matrix_t(n_cmp, n_chunk, S // SEL_BLOCK)
    o_cmp, sel = _cmp_select(q, kc, vct, ovt, gate)
    o_slc = _flash_gqa(q, ks, vst, mode="causal", sel=sel, gate=gate, gate_base=8, tq=256)
    o_win = _flash_gqa(q, kw, vwt, mode="band", window=NSA_WINDOW, gate=gate, gate_base=16, nsub=4)
    return [o_cmp, o_slc, o_win], z, _OPERM


def kernel(x, p, positions, w_out, ln_g, ln_b, pe_gate, pe_proj, l0_w_in, l0_sinks, l1_w_in, l1_q_norm, l1_kv_norm, l1_w_uq, l1_w_ukv, l2_w_in, l2_cmp_pos, l2_phi_k1, l2_phi_k2, l2_phi_v1, l2_phi_v2, l3_w_in, l3_sinks):
    B, S, _ = x.shape
    pos_flat = positions.reshape(-1)
    tabs64 = [t.reshape(B, S, LANES) for t in _rope_tables(pos_flat, HEAD_DIM // 2)]
    tabs32 = [t.reshape(B, S, LANES) for t in _rope_tables(pos_flat, MLA_ROPE // 2)]
    layers = (
        lambda h: _layer_swa(h, tabs64, l0_w_in, l0_sinks),
        lambda h: _layer_mla(h, tabs32, l1_w_in, l1_q_norm, l1_kv_norm, l1_w_uq, l1_w_ukv),
        lambda h: _layer_nsa(h, tabs64, positions, l2_w_in, l2_cmp_pos, l2_phi_k1, l2_phi_k2, l2_phi_v1, l2_phi_v2),
        lambda h: _layer_swa(h, tabs64, l3_w_in, l3_sinks),
    )
    for i, layer in enumerate(layers):
        o_list, z, operm = layer(x)
        wo = w_out[i] if operm is None else w_out[i][operm, :]
        x = _out_block(o_list, z, x, p, i, wo.astype(BF16), ln_g[i][None, :], ln_b[i][None, :],
                       pe_gate[i].astype(BF16), pe_proj[i].astype(BF16))
    return x
```

```python
import functools

import numpy as np
import jax
import jax.numpy as jnp
from jax import lax
from jax.experimental import pallas as pl
from jax.experimental.pallas import tpu as pltpu

F32 = jnp.float32
BF16 = jnp.bfloat16

LANES = 128
VMEM_LIMIT = 56 * 1024 * 1024

D_MODEL = 1024
DEPTH = 4
HEAD_DIM = 64
N_HEADS = 16
MIX_WIDTH = 1024
ROPE_THETA = 10000.0
NEG_INF = -1e30
PLE_DIM = 256
DEEPNORM_ALPHA = (2 * DEPTH) ** 0.25
LN_EPS = 1e-5
RMS_EPS = 1e-6

SWA_WINDOW = 128
KV_GROUPS = 4
KV_WIDTH = KV_GROUPS * HEAD_DIM

MLA_Q_LORA = 384
MLA_KV_LORA = 256
MLA_NOPE = 64
MLA_ROPE = 32
MLA_V = 64

CMP_BLOCK = 32
CMP_STRIDE = 16
SEL_BLOCK = 64
N_SEL = 8
SEL_FORCE = 1e9
NSA_WINDOW = 512
PHI_HIDDEN = 256

BAND_BLOCK = 128
OUT_ROWS = 128
LOG2E = 1.4426950408889634


def _cparams(sem, flags=None):
    return pltpu.CompilerParams(dimension_semantics=sem, vmem_limit_bytes=VMEM_LIMIT, flags=flags)


_ATTN_FLAGS = None


def _pair_head(gp, r, half):
    return (2 * gp + half) * 4 + r


def _gqa_perms():
    qperm, kperm, operm = [], [], []
    for gp in range(2):
        for r in range(4):
            for part in range(2):
                for half in range(2):
                    h = _pair_head(gp, r, half)
                    qperm += [h * 64 + part * 32 + i for i in range(32)]
            for half in range(2):
                h = _pair_head(gp, r, half)
                operm += [h * 64 + d for d in range(64)]
        for part in range(2):
            for half in range(2):
                g = 2 * gp + half
                kperm += [g * 64 + part * 32 + i for i in range(32)]
    return np.array(qperm), np.array(kperm), np.array(operm)


_QPERM, _KPERM, _OPERM = _gqa_perms()
_POS_HEAD = np.array([_pair_head(gp, r, half) for gp in range(2) for r in range(4) for half in range(2)])


def _mla_perms():
    uq, kr = [], []
    for hg in range(4):
        for a in range(4):
            uq += [(4 * hg + a) * 96 + d for d in range(64)]
        for part in range(2):
            for a in range(4):
                uq += [(4 * hg + a) * 96 + 64 + part * 16 + i for i in range(16)]
    for part in range(2):
        for _ in range(4):
            kr += [part * 16 + i for i in range(16)]
    kn = [h * 128 + d for h in range(16) for d in range(64)]
    vv = [h * 128 + 64 + d for h in range(16) for d in range(64)]
    return np.array(uq), np.array(kr), np.array(kn), np.array(vv)


_UQPERM, _KRPERM, _KNPERM, _VPERM = _mla_perms()


def _rope_lane_tables(half):
    inv = ROPE_THETA ** (-jnp.arange(half, dtype=F32) / half)
    lane = np.arange(LANES)
    inv_l = inv[lane % half][None, :]
    sign = np.where(lane < 64, -1.0, 1.0).astype(np.float32)[None, :]
    return inv_l, jnp.asarray(sign)


def _rope_table_body(pos_ref, inv_ref, sign_ref, cos_ref, sin_ref):
    ang = pos_ref[...].astype(F32) * inv_ref[...]
    cos_ref[...] = jnp.cos(ang)
    sin_ref[...] = jnp.sin(ang) * sign_ref[...]


def _rope_tables(pos, half):
    n = pos.shape[0]
    tr = min(n, 1024)
    inv_l, sign = _rope_lane_tables(half)
    row = pl.BlockSpec((tr, LANES), lambda i: (i, 0))
    cst = pl.BlockSpec((1, LANES), lambda i: (0, 0))
    return pl.pallas_call(
        _rope_table_body,
        grid=(n // tr,),
        in_specs=[pl.BlockSpec((tr, 1), lambda i: (i, 0)), cst, cst],
        out_specs=[row, row],
        out_shape=[jax.ShapeDtypeStruct((n, LANES), F32)] * 2,
        compiler_params=_cparams(("parallel",)),
        name="rope_tables",
    )(pos.reshape(n, 1), inv_l, sign)


def _rope_cols(v, cos, sin):
    return v * cos + pltpu.roll(v, 64, 1) * sin


def _proj_gqa_body(n_kv, with_cmp, x_ref, w_ref, cos_ref, sin_ref, *outs):
    xb = x_ref[0].astype(BF16)
    cos = cos_ref[0]
    sin = sin_ref[0]

    def mm(c0, width):
        return jnp.dot(xb, w_ref[:, c0:c0 + width], preferred_element_type=F32)

    q_ref, z_ref = outs[0], outs[1]
    for c in range(0, MIX_WIDTH, 512):
        acc = mm(c, 512) * (HEAD_DIM ** -0.5 * LOG2E)
        for j in range(0, 512, LANES):
            q_ref[0, :, c + j:c + j + LANES] = _rope_cols(acc[:, j:j + LANES], cos, sin).astype(BF16)
    for c in range(0, MIX_WIDTH, 512):
        z_ref[0, :, c:c + 512] = mm(MIX_WIDTH + c, 512).astype(BF16)
    col = 2 * MIX_WIDTH
    o = 2
    for _ in range(n_kv):
        acc = mm(col, 2 * KV_WIDTH)
        for j in range(0, KV_WIDTH, LANES):
            outs[o][0, :, j:j + LANES] = _rope_cols(acc[:, j:j + LANES], cos, sin).astype(BF16)
        outs[o + 1][0] = acc[:, KV_WIDTH:].T.astype(BF16)
        col += 2 * KV_WIDTH
        o += 2
    if with_cmp:
        acc = mm(col, 2 * KV_WIDTH)
        for t in range(2):
            for gp in range(2):
                c0 = t * KV_WIDTH + gp * LANES
                outs[o + t][0, gp] = acc[:, c0:c0 + LANES]
        col += 2 * KV_WIDTH
        outs[o + 2][0] = jax.nn.sigmoid(mm(col, 2 * LANES))


def _proj_gqa(x, w, cos, sin, n_kv, with_cmp, tm=512):
    B, S, _ = x.shape
    n = w.shape[1]
    tok = lambda width: pl.BlockSpec((1, tm, width), lambda b, i: (b, i, 0))
    out_specs = [tok(MIX_WIDTH), tok(MIX_WIDTH)]
    out_shape = [jax.ShapeDtypeStruct((B, S, MIX_WIDTH), BF16)] * 2
    for _ in range(n_kv):
        out_specs += [tok(KV_WIDTH), pl.BlockSpec((1, KV_WIDTH, tm), lambda b, i: (b, 0, i))]
        out_shape += [jax.ShapeDtypeStruct((B, S, KV_WIDTH), BF16), jax.ShapeDtypeStruct((B, KV_WIDTH, S), BF16)]
    if with_cmp:
        raw = pl.BlockSpec((1, 2, tm, LANES), lambda b, i: (b, 0, i, 0))
        out_specs += [raw, raw, tok(2 * LANES)]
        out_shape += [jax.ShapeDtypeStruct((B, 2, S, LANES), F32)] * 2
        out_shape += [jax.ShapeDtypeStruct((B, S, 2 * LANES), F32)]
    return pl.pallas_call(
        functools.partial(_proj_gqa_body, n_kv, with_cmp),
        grid=(B, S // tm),
        in_specs=[tok(D_MODEL), pl.BlockSpec((D_MODEL, n), lambda b, i: (0, 0)), tok(LANES), tok(LANES)],
        out_specs=out_specs,
        out_shape=out_shape,
        compiler_params=_cparams(("parallel", "parallel")),
        name="proj_gqa",
    )(x, w, cos, sin)


def _rms(v, g):
    return v * lax.rsqrt(jnp.mean(v * v, -1, keepdims=True) + RMS_EPS) * g


def _proj_mla_body(x_ref, w1_ref, qn_ref, kvn_ref, wuq_ref, wkv_ref, cos_ref, sin_ref,
                   q_ref, k_ref, v_ref, kr_ref, z_ref):
    xb = x_ref[0].astype(BF16)
    cos = cos_ref[0]
    sin = sin_ref[0]
    lat = MLA_Q_LORA + MLA_KV_LORA
    c = jnp.dot(xb, w1_ref[:, :lat + LANES], preferred_element_type=F32)
    cq = _rms(c[:, :MLA_Q_LORA], qn_ref[...]).astype(BF16)
    ckv = _rms(c[:, MLA_Q_LORA:lat], kvn_ref[...]).astype(BF16)
    kr_ref[0] = _rope_cols(c[:, lat:lat + LANES], cos, sin).astype(BF16)
    scale = (MLA_NOPE + MLA_ROPE) ** -0.5 * LOG2E
    for hg in range(4):
        c0 = hg * 384
        q = jnp.dot(cq, wuq_ref[:, c0:c0 + 384], preferred_element_type=F32) * scale
        q_ref[0, :, c0:c0 + 256] = q[:, :256].astype(BF16)
        q_ref[0, :, c0 + 256:c0 + 384] = _rope_cols(q[:, 256:], cos, sin).astype(BF16)
    for c0 in range(0, MIX_WIDTH, 512):
        k_ref[0, :, c0:c0 + 512] = jnp.dot(ckv, wkv_ref[:, c0:c0 + 512], preferred_element_type=F32).astype(BF16)
        v_ref[0, c0:c0 + 512, :] = jnp.dot(ckv, wkv_ref[:, MIX_WIDTH + c0:MIX_WIDTH + c0 + 512],
                                            preferred_element_type=F32).T.astype(BF16)
        z_ref[0, :, c0:c0 + 512] = jnp.dot(xb, w1_ref[:, lat + LANES + c0:lat + LANES + c0 + 512],
                                            preferred_element_type=F32).astype(BF16)


def _proj_mla(x, w1, qn, kvn, wuq, wkv, cos, sin, tm=512):
    B, S, _ = x.shape
    tok = lambda width: pl.BlockSpec((1, tm, width), lambda b, i: (b, i, 0))
    full = lambda a: pl.BlockSpec(a.shape, lambda b, i: (0,) * a.ndim)
    widths = [16 * 96, MIX_WIDTH, MIX_WIDTH, LANES, MIX_WIDTH]
    out_specs = [tok(w) for w in widths]
    out_shape = [jax.ShapeDtypeStruct((B, S, w), BF16) for w in widths]
    out_specs[2] = pl.BlockSpec((1, MIX_WIDTH, tm), lambda b, i: (b, 0, i))
    out_shape[2] = jax.ShapeDtypeStruct((B, MIX_WIDTH, S), BF16)
    return pl.pallas_call(
        _proj_mla_body,
        grid=(B, S // tm),
        in_specs=[tok(D_MODEL), full(w1), full(qn), full(kvn), full(wuq), full(wkv), tok(LANES), tok(LANES)],
        out_specs=out_specs,
        out_shape=out_shape,
        compiler_params=_cparams(("parallel", "parallel")),
        name="proj_mla",
    )(x, w1, qn, kvn, wuq, wkv, cos, sin)


_NT = (((1,), (1,)), ((), ()))


def _lane_iota():
    return lax.broadcasted_iota(jnp.int32, (1, LANES), 1)


def _first_half_rows():
    return lax.broadcasted_iota(jnp.int32, (LANES, 1), 0) < 64


def _pair_lhs(q_cols, extra_a=None, extra_b=None):
    m_a = ((_lane_iota() >> 5) & 1) == 0
    a_rows, b_rows = [], []
    for qc in q_cols:
        a = jnp.where(m_a, qc, 0)
        b = jnp.where(m_a, 0, qc)
        if extra_a is not None:
            a = jnp.concatenate([a, extra_a], 1)
            b = jnp.concatenate([b, extra_b], 1)
        a_rows.append(a)
        b_rows.append(b)
    return jnp.concatenate(a_rows + b_rows, 0)


def _tile_lanes(bias, reps):
    return jnp.concatenate([bias] * reps, 1)


def _online_step(lhs, kt, vt, bias, state):
    return _softmax_pv(_scores(lhs, kt), vt, bias, state)


def _scores(lhs, kt):
    return lax.dot_general(kt, lhs, _NT, preferred_element_type=F32)


def _softmax_pv(s, vt, bias, state):
    m, acc_a, acc_b = state
    half = s.shape[1] // 2
    if bias is not None:
        s = s + bias
    m_new = jnp.maximum(m, jnp.max(s, 0, keepdims=True))
    p = jnp.exp2(s - m_new).astype(BF16)
    top = _first_half_rows()
    pv_a = jnp.dot(jnp.where(top, vt, 1), p[:, :half], preferred_element_type=F32)
    pv_b = jnp.dot(jnp.where(top, 1, vt), p[:, half:], preferred_element_type=F32)
    if acc_a is None:
        return m_new, pv_a, pv_b
    alpha = jnp.exp2(m - m_new)
    return m_new, acc_a * alpha[:, :half] + pv_a, acc_b * alpha[:, half:] + pv_b


def _causal_sweep(score_fn, pv_fn, n_chain, n_full, tk, bias, init, buf_a, buf_b):
    def scores_to(buf, start, bias=None):
        start = pl.multiple_of(start, BAND_BLOCK)
        for c in range(n_chain):
            s = score_fn(c, start)
            buf[c] = s if bias is None else s + bias

    def consume(buf, start, states):
        start = pl.multiple_of(start, BAND_BLOCK)
        return tuple(pv_fn(c, buf[c], start, states[c]) for c in range(n_chain))

    diag_start = n_full * tk
    scores_to(buf_a, diag_start, bias)

    def two_tiles(i, carry):
        states, prev = carry
        scores_to(buf_b, 2 * i * tk)
        states = consume(buf_a, prev, states)
        scores_to(buf_a, (2 * i + 1) * tk)
        states = consume(buf_b, 2 * i * tk, states)
        return states, (2 * i + 1) * tk

    states, prev = lax.fori_loop(0, n_full // 2, two_tiles, (init, diag_start))

    def odd_tail(states):
        last = (n_full - 1) * tk
        scores_to(buf_b, last)
        return consume(buf_b, last, consume(buf_a, prev, states))

    return lax.cond(n_full % 2 == 1, odd_tail, lambda st: consume(buf_a, prev, st), states)


def _finish(state, gate_rows):
    _, acc_a, acc_b = state
    half = acc_a.shape[1]
    inv_a = 1.0 / acc_a[64:65, :]
    inv_b = 1.0 / acc_b[0:1, :]
    if gate_rows is not None:
        inv_a = inv_a * gate_rows[:, :half]
        inv_b = inv_b * gate_rows[:, half:]
    return jnp.where(_first_half_rows(), acc_a * inv_a, acc_b * inv_b)


def _gate_rows(gate_t, cols):
    return jnp.concatenate([gate_t[c:c + 1, :] for c in cols], 1)


def _flash_gqa_body(cfg, *refs):
    tq, nsub, n_prev, mode, use_sel, use_sink, gate_base, tk = cfg
    refs = list(refs)
    q_ref, k_ref, vt_ref, bias_ref = refs[:4]
    nxt = 4
    sb_ref = gate_ref = sink_ref = None
    if use_sel:
        sb_ref = refs[nxt]; nxt += 1
    if gate_base is not None:
        gate_ref = refs[nxt]; nxt += 1
    if use_sink:
        sink_ref = refs[nxt]; nxt += 1
    o_ref = refs[nxt]

    step_i = pl.program_id(1)
    rows = 8 * tq
    top = _first_half_rows()
    chains = [(sub, gp) for sub in range(nsub) for gp in range(2)]
    lhs, init = [], []
    for sub, gp in chains:
        tok = slice(sub * tq, (sub + 1) * tq)
        q_cols = [q_ref[0, tok, (4 * gp + r) * LANES:(4 * gp + r + 1) * LANES] for r in range(4)]
        if use_sel:
            lhs.append(_pair_lhs(q_cols, sb_ref[0, tok, 2 * gp * LANES:(2 * gp + 1) * LANES],
                                 sb_ref[0, tok, (2 * gp + 1) * LANES:(2 * gp + 2) * LANES]))
        else:
            lhs.append(_pair_lhs(q_cols))
        if use_sink:
            m0 = jnp.concatenate([jnp.full((1, tq), sink_ref[_pair_head(gp, r, half)] * LOG2E, F32)
                                  for half in range(2) for r in range(4)], 1)
            init.append((m0, jnp.where(top, jnp.zeros((LANES, rows // 2), F32), 1.0),
                         jnp.where(top, jnp.ones((LANES, rows // 2), F32), 0.0)))
        else:
            init.append((jnp.full((1, rows), NEG_INF, F32), None, None))

    def step(starts, states, biases):
        starts = [pl.multiple_of(st, BAND_BLOCK) for st in starts]
        s = []
        for c, (sub, gp) in enumerate(chains):
            kt = k_ref[0, pl.ds(starts[c], tk), gp * LANES:(gp + 1) * LANES]
            if use_sel:
                blk = (starts[c] + lax.broadcasted_iota(jnp.int32, (tk, LANES), 0)) >> 6
                onehot = (blk == lax.broadcasted_iota(jnp.int32, (tk, LANES), 1)).astype(BF16)
                kt = jnp.concatenate([kt, onehot], 1)
            s.append(_scores(lhs[c], kt))
        if mode == "band":
            buf = refs[nxt + 1]
            for c in range(len(chains)):
                buf[c] = s[c]
            s = [buf[c] for c in range(len(chains))]
        return tuple(
            _softmax_pv(s[c], vt_ref[0, gp * LANES:(gp + 1) * LANES, pl.ds(starts[c], tk)], biases[c], states[c])
            for c, (sub, gp) in enumerate(chains))

    if mode == "band":
        tiles = [step_i * nsub + sub for sub, gp in chains]
        states = step([jnp.maximum(t * tq - (tk - tq), 0) for t in tiles], init,
                      [_tile_lanes(bias_ref[jnp.minimum(t, n_prev)], 8) for t in tiles])
    else:
        buf_a, buf_b = refs[nxt + 1], refs[nxt + 2]

        def score_fn(c, start):
            gp = chains[c][1]
            kt = k_ref[0, pl.ds(start, tk), gp * LANES:(gp + 1) * LANES]
            if use_sel:
                blk = (start + lax.broadcasted_iota(jnp.int32, (tk, LANES), 0)) >> 6
                onehot = (blk == lax.broadcasted_iota(jnp.int32, (tk, LANES), 1)).astype(BF16)
                kt = jnp.concatenate([kt, onehot], 1)
            return _scores(lhs[c], kt)

        def pv_fn(c, s, start, state):
            gp = chains[c][1]
            return _softmax_pv(s, vt_ref[0, gp * LANES:(gp + 1) * LANES, pl.ds(start, tk)], None, state)

        zeros = jnp.zeros((LANES, rows // 2), F32)
        states = _causal_sweep(score_fn, pv_fn, len(chains), (step_i * tq) // tk, tk,
                               _tile_lanes(bias_ref[step_i % (tk // tq)], 8),
                               tuple((m0, zeros, zeros) for m0, _, _ in init), buf_a, buf_b)

    for c, (sub, gp) in enumerate(chains):
        tok = slice(sub * tq, (sub + 1) * tq)
        gate_rows = None
        if gate_base is not None:
            gate_t = gate_ref[0, tok, gp * LANES:(gp + 1) * LANES].T
            gate_rows = _gate_rows(gate_t, [gate_base + 2 * r for r in range(4)]
                                   + [gate_base + 2 * r + 1 for r in range(4)])
        out_t = _finish(states[c], gate_rows)
        for r in range(4):
            o_ref[0, tok, (4 * gp + r) * LANES:(4 * gp + r + 1) * LANES] = (
                out_t[:, r * tq:(r + 1) * tq].T.astype(o_ref.dtype))


def _band_bias(window, tq, tk):
    pad = tk - tq
    r = np.arange(tk)[:, None]
    c = np.arange(tq)[None, :]
    out = []
    for v in range(pad // tq + 1):
        d = r - c - (v * tq if v < pad // tq else pad)
        out.append(np.where((d <= 0) & (d > -window), 0.0, NEG_INF))
    return jnp.asarray(np.stack(out), F32)


def _diag_bias(tq, tk):
    r = np.arange(tk)[:, None]
    c = np.arange(tq)[None, :]
    return jnp.asarray(np.stack([np.where(r <= v * tq + c, 0.0, NEG_INF) for v in range(tk // tq)]), F32)


def _flash_gqa(q, k, vt, *, mode, window=None, sel=None, gate=None, gate_base=None, sinks=None,
               tq=128, nsub=1, tk=512):
    B, S, _ = q.shape
    n_prev = 0
    if mode == "band":
        pad = -(-(window - 1) // BAND_BLOCK) * BAND_BLOCK
        tk = pad + tq
        bias = _band_bias(window, tq, tk)
        n_prev = pad // tq
    else:
        assert nsub == 1
        bias = _diag_bias(tq, tk)
    bias_spec = pl.BlockSpec(bias.shape, lambda b, i: (0, 0, 0))
    cfg = (tq, nsub, n_prev, mode, sel is not None, sinks is not None, gate_base if gate is not None else None, tk)
    tok = lambda width: pl.BlockSpec((1, nsub * tq, width), lambda b, i: (b, i, 0))
    in_specs = [tok(MIX_WIDTH), pl.BlockSpec((1, S, KV_WIDTH), lambda b, i: (b, 0, 0)),
                pl.BlockSpec((1, KV_WIDTH, S), lambda b, i: (b, 0, 0)), bias_spec]
    args = [q, k, vt, bias]
    if sel is not None:
        in_specs.append(tok(4 * LANES))
        args.append(sel)
    if gate is not None:
        in_specs.append(tok(2 * LANES))
        args.append(gate)
    if sinks is not None:
        in_specs.append(pl.BlockSpec(memory_space=pltpu.SMEM))
        args.append(sinks)
    return pl.pallas_call(
        functools.partial(_flash_gqa_body, cfg),
        grid=(B, S // (nsub * tq)),
        in_specs=in_specs,
        out_specs=tok(MIX_WIDTH),
        out_shape=jax.ShapeDtypeStruct((B, S, MIX_WIDTH), BF16),
        scratch_shapes=([pltpu.VMEM((2 * nsub, tk, 8 * tq), F32)] if mode == "band"
                        else [pltpu.VMEM((2, tk, 8 * tq), F32)] * 2),
        compiler_params=_cparams(("parallel", "arbitrary"), _ATTN_FLAGS),
        name=f"attn_{mode}{'_sel' if sel is not None else ''}{window or ''}",
    )(*args)


def _flash_mla_body(cfg, q_ref, k_ref, vt_ref, kr_ref, bias_ref, o_ref, buf_a, buf_b):
    tq, nhg, tk = cfg
    step_i = pl.program_id(2)
    q0 = step_i * tq
    lane = _lane_iota()
    lo = lane < 64
    slot = (lane >> 4) & 3
    n_pair = 2 * nhg
    lhs = []
    for hg in range(nhg):
        qr = q_ref[0, :, hg * 384 + 256:hg * 384 + 384]
        for pair in range(2):
            qn = q_ref[0, :, hg * 384 + pair * LANES:hg * 384 + (pair + 1) * LANES]
            a = jnp.concatenate([jnp.where(lo, qn, 0), jnp.where(slot == 2 * pair, qr, 0)], 1)
            b = jnp.concatenate([jnp.where(lo, 0, qn), jnp.where(slot == 2 * pair + 1, qr, 0)], 1)
            lhs.append(jnp.concatenate([a, b], 0))
    rows = 2 * tq
    init = (jnp.full((1, rows), NEG_INF, F32), jnp.zeros((LANES, tq), F32), jnp.zeros((LANES, tq), F32))

    def score_fn(c, start):
        kt = jnp.concatenate([k_ref[0, pl.ds(start, tk), c * LANES:(c + 1) * LANES],
                              kr_ref[0, pl.ds(start, tk), :]], 1)
        return _scores(lhs[c], kt)

    def pv_fn(c, s, start, state):
        return _softmax_pv(s, vt_ref[0, c * LANES:(c + 1) * LANES, pl.ds(start, tk)], None, state)

    states = _causal_sweep(score_fn, pv_fn, n_pair, q0 // tk, tk, _tile_lanes(bias_ref[step_i % (tk // tq)], 2),
                           (init,) * n_pair, buf_a, buf_b)
    for c in range(n_pair):
        o_ref[0, :, c * LANES:(c + 1) * LANES] = _finish(states[c], None).T.astype(o_ref.dtype)


def _flash_mla(q, k, vt, kr, tq=512, tk=512, nhg=2):
    B, S, _ = q.shape
    bias = _diag_bias(tq, tk)
    return pl.pallas_call(
        functools.partial(_flash_mla_body, (tq, nhg, tk)),
        grid=(B, 4 // nhg, S // tq),
        in_specs=[pl.BlockSpec((1, tq, nhg * 384), lambda b, g, i: (b, i, g)),
                  pl.BlockSpec((1, S, nhg * 256), lambda b, g, i: (b, 0, g)),
                  pl.BlockSpec((1, nhg * 256, S), lambda b, g, i: (b, g, 0)),
                  pl.BlockSpec((1, S, LANES), lambda b, g, i: (b, 0, 0)),
                  pl.BlockSpec(bias.shape, lambda b, g, i: (0, 0, 0))],
        out_specs=pl.BlockSpec((1, tq, nhg * 256), lambda b, g, i: (b, i, g)),
        out_shape=jax.ShapeDtypeStruct((B, S, MIX_WIDTH), BF16),
        scratch_shapes=[pltpu.VMEM((2 * nhg, tk, 2 * tq), F32)] * 2,
        compiler_params=_cparams(("parallel", "parallel", "arbitrary"), _ATTN_FLAGS),
        name="attn_mla",
    )(q, k, vt, kr, bias)


def _compress_body(kraw_ref, vraw_ref, posa_ref, posb_ref, wk1a_ref, wk1b_ref, wk2_ref,
                   wv1a_ref, wv1b_ref, wv2_ref, cos_ref, sin_ref, kc_ref, vct_ref):
    n = kraw_ref.shape[2] // CMP_STRIDE

    def chunks(ref):
        return jnp.concatenate([ref[0, 0, pl.ds(t, n, stride=CMP_STRIDE), :] for t in range(CMP_STRIDE)], 1)

    def phi(raw, w1a, w1b, w2):
        a = jnp.dot((raw + posa_ref[...]).astype(BF16), w1a[...], preferred_element_type=F32)
        b = jnp.dot((raw + posb_ref[...]).astype(BF16), w1b[...], preferred_element_type=F32)
        h = a + pltpu.roll(b, n - 1, 0)
        return jnp.dot(jax.nn.silu(h).astype(BF16), w2[...], preferred_element_type=F32)

    kc = phi(chunks(kraw_ref), wk1a_ref, wk1b_ref, wk2_ref)
    kc_ref[0, 0] = _rope_cols(kc, cos_ref[0], sin_ref[0]).astype(BF16)
    vct_ref[0, 0] = phi(chunks(vraw_ref), wv1a_ref, wv1b_ref, wv2_ref).T.astype(BF16)


def _compress(kraw, vraw, posa, posb, wk, wv, cos_c, sin_c):
    B, _, S, _ = kraw.shape
    n = S // CMP_STRIDE
    raw = pl.BlockSpec((1, 1, S, LANES), lambda b, g: (b, g, 0, 0))
    full = lambda a: pl.BlockSpec(a.shape, lambda b, g: (0,) * a.ndim)
    tab = pl.BlockSpec((1, n, LANES), lambda b, g: (b, 0, 0))
    return pl.pallas_call(
        _compress_body,
        grid=(B, 2),
        in_specs=[raw, raw, full(posa), full(posb)] + [full(w) for w in wk] + [full(w) for w in wv] + [tab, tab],
        out_specs=[pl.BlockSpec((1, 1, n, LANES), lambda b, g: (b, g, 0, 0)),
                   pl.BlockSpec((1, 1, LANES, n), lambda b, g: (b, g, 0, 0))],
        out_shape=[jax.ShapeDtypeStruct((B, 2, n, LANES), BF16), jax.ShapeDtypeStruct((B, 2, LANES, n), BF16)],
        compiler_params=_cparams(("parallel", "parallel")),
        name="nsa_compress",
    )(kraw, vraw, posa, posb, *wk, *wv, cos_c, sin_c)


def _split3(p):
    hi = p.astype(BF16)
    r1 = p - hi.astype(F32)
    mid = r1.astype(BF16)
    lo = (r1 - mid.astype(F32)).astype(BF16)
    return hi, mid, lo


def _cmp_select_body(cfg, q_ref, kc_ref, vct_ref, ovt_ref, gate_ref, o_ref, sb_ref, s_buf):
    tq, n_cmp_pad, n_blk = cfg
    q0 = pl.program_id(1) * tq
    rows = 8 * tq
    half_rows = rows // 2
    tpos = q0 + (lax.broadcasted_iota(jnp.int32, (1, rows), 1) & (tq - 1))
    blk_end = lax.broadcasted_iota(jnp.int32, (n_cmp_pad, 1), 0) * CMP_STRIDE + (CMP_BLOCK - 1)
    cbias = _tile_lanes(jnp.where(blk_end <= tpos[:, :tq], 0.0, NEG_INF), 8)
    sees_any = tpos >= CMP_BLOCK - 1
    top = _first_half_rows()
    jrow = lax.broadcasted_iota(jnp.int32, (n_blk, 1), 0)
    sub8 = lax.broadcasted_iota(jnp.int32, (8, 1), 0)
    cur = (q0 + lax.broadcasted_iota(jnp.int32, (1, tq), 1)) >> 6
    forced = (jrow == 0) | (jrow == cur) | (jrow == cur - 1)
    ovt = ovt_ref[...]

    for gp in range(2):
        lhs = _pair_lhs([q_ref[0, :, (4 * gp + r) * LANES:(4 * gp + r + 1) * LANES] for r in range(4)])
        s_buf[gp] = lax.dot_general(kc_ref[0, gp], lhs, _NT, preferred_element_type=F32)

    for gp in range(2):
        s = s_buf[gp] + cbias
        m = jnp.max(s, 0, keepdims=True)
        p = jnp.exp2(s - m)
        l = jnp.sum(p, 0, keepdims=True)
        pc = p * jnp.where(sees_any, 1.0 / l, 0.0)

        vct = vct_ref[0, gp]
        pb = pc.astype(BF16)
        pv = (jnp.dot(jnp.where(top, vct, 0), pb[:, :half_rows], preferred_element_type=F32)
              + jnp.dot(jnp.where(top, 0, vct), pb[:, half_rows:], preferred_element_type=F32))
        gate_t = gate_ref[0, :, gp * LANES:(gp + 1) * LANES].T
        ga = _gate_rows(gate_t, [2 * r for r in range(4)])
        gb = _gate_rows(gate_t, [2 * r + 1 for r in range(4)])
        out_t = pv * jnp.where(top, ga, gb)
        for r in range(4):
            o_ref[0, :, (4 * gp + r) * LANES:(4 * gp + r + 1) * LANES] = (
                out_t[:, r * tq:(r + 1) * tq].T.astype(o_ref.dtype))

        for half in range(2):
            c0 = 4 * half * tq
            psum = pc[:, c0:c0 + tq] + pc[:, c0 + tq:c0 + 2 * tq] + pc[:, c0 + 2 * tq:c0 + 3 * tq] + pc[:, c0 + 3 * tq:c0 + 4 * tq]
            imp = sum(jnp.dot(ovt, part, preferred_element_type=F32) for part in _split3(psum))[:n_blk]
            score = jnp.where(forced, SEL_FORCE, jnp.where(jrow <= cur, imp, -SEL_FORCE))
            groups = [score[g:g + 8] for g in range(0, n_blk, 8)]
            ranks = [jnp.zeros((8, tq), F32) for _ in groups]
            for j in range(n_blk):
                row = score[j:j + 1, :]
                for gi, grp in enumerate(groups):
                    if j < 8 * gi:
                        beats = row >= grp
                    elif j >= 8 * gi + 8:
                        beats = row > grp
                    else:
                        beats = (row > grp) | ((row == grp) & (sub8 > j - 8 * gi))
                    ranks[gi] = ranks[gi] + jnp.where(beats, 1.0, 0.0)
            rank = jnp.concatenate(ranks, 0)
            bias = jnp.where(rank < min(N_SEL, n_blk), 0.0, NEG_INF)
            bias = jnp.concatenate([bias, jnp.zeros((LANES - n_blk, tq), F32)], 0)
            sb_ref[0, :, (2 * gp + half) * LANES:(2 * gp + half + 1) * LANES] = bias.T.astype(BF16)


def _cmp_select(q, kc, vct, ovt, gate, tq=512):
    B, S, _ = q.shape
    n_cmp_pad = kc.shape[2]
    n_blk = S // SEL_BLOCK
    tok = lambda width: pl.BlockSpec((1, tq, width), lambda b, i: (b, i, 0))
    return pl.pallas_call(
        functools.partial(_cmp_select_body, (tq, n_cmp_pad, n_blk)),
        grid=(B, S // tq),
        in_specs=[tok(MIX_WIDTH), pl.BlockSpec((1, 2, n_cmp_pad, LANES), lambda b, i: (b, 0, 0, 0)),
                  pl.BlockSpec((1, 2, LANES, n_cmp_pad), lambda b, i: (b, 0, 0, 0)),
                  pl.BlockSpec(ovt.shape, lambda b, i: (0, 0)), tok(2 * LANES)],
        out_specs=[tok(MIX_WIDTH), tok(4 * LANES)],
        out_shape=[jax.ShapeDtypeStruct((B, S, MIX_WIDTH), BF16), jax.ShapeDtypeStruct((B, S, 4 * LANES), BF16)],
        scratch_shapes=[pltpu.VMEM((2, n_cmp_pad, 8 * tq), F32)],
        compiler_params=_cparams(("parallel", "parallel")),
        name="nsa_cmp_select",
    )(q, kc, vct, ovt, gate)


def _out_body(n_o, *refs):
    o_refs = refs[:n_o]
    z_ref, x_ref, p_ref, wo_ref, g_ref, b_ref, wg_ref, wp_ref, out_ref = refs[n_o:]
    tm = x_ref.shape[1]
    blocks = [slice(r0, r0 + OUT_ROWS) for r0 in range(0, tm, OUT_ROWS)]
    gated = []
    for rs in blocks:
        o = o_refs[0][0, rs].astype(F32)
        for r in o_refs[1:]:
            o = o + r[0, rs].astype(F32)
        gated.append((o * jax.nn.silu(z_ref[0, rs].astype(F32))).astype(BF16))
    y = [jnp.dot(g, wo_ref[...], preferred_element_type=F32) for g in gated]
    emb = [jnp.dot(p_ref[0, 0, rs].astype(BF16), wp_ref[...], preferred_element_type=F32) for rs in blocks]
    xn = []
    for rs, yb in zip(blocks, y):
        u = DEEPNORM_ALPHA * x_ref[0, rs] + yb
        mu = jnp.mean(u, -1, keepdims=True)
        var = jnp.mean(jnp.square(u - mu), -1, keepdims=True)
        xn.append((u - mu) * lax.rsqrt(var + LN_EPS) * g_ref[...] + b_ref[...])
    gate = [jax.nn.sigmoid(jnp.dot(v.astype(BF16), wg_ref[...], preferred_element_type=F32)) for v in xn]
    for rs, v, gt, e in zip(blocks, xn, gate, emb):
        out_ref[0, rs] = v + gt * e


def _out_block(o_list, z, x, p, layer, wo, g, b, wg, wp, tm=512):
    B, S, _ = x.shape
    tok = lambda width: pl.BlockSpec((1, tm, width), lambda bb, i: (bb, i, 0))
    full = lambda a: pl.BlockSpec(a.shape, lambda bb, i: (0,) * a.ndim)
    n_o = len(o_list)
    return pl.pallas_call(
        functools.partial(_out_body, n_o),
        grid=(B, S // tm),
        in_specs=[tok(MIX_WIDTH)] * n_o + [tok(MIX_WIDTH), tok(D_MODEL),
                                           pl.BlockSpec((1, 1, tm, PLE_DIM), lambda bb, i: (layer, bb, i, 0)),
                                           full(wo), full(g), full(b), full(wg), full(wp)],
        out_specs=tok(D_MODEL),
        out_shape=jax.ShapeDtypeStruct((B, S, D_MODEL), F32),
        compiler_params=_cparams(("parallel", "parallel")),
        name="out_block",
    )(*o_list, z, x, p, wo, g, b, wg, wp)


def _swa_weights(w_in):
    q, k, v, z = jnp.split(w_in, [MIX_WIDTH, MIX_WIDTH + KV_WIDTH, MIX_WIDTH + 2 * KV_WIDTH], axis=1)
    return jnp.concatenate([q[:, _QPERM], z[:, _OPERM], k[:, _KPERM], v], 1).astype(BF16)


def _nsa_weights(w_in):
    cuts = np.cumsum([MIX_WIDTH] + [KV_WIDTH] * 6 + [3 * N_HEADS]).tolist()
    q, k_c, v_c, k_s, v_s, k_w, v_w, gl, z = jnp.split(w_in, cuts, axis=1)
    q = q[:, _QPERM]
    gl = jnp.concatenate(
        [jnp.pad(gl[:, np.array([br * N_HEADS + _POS_HEAD[gp * 8 + c] for br in range(3) for c in range(8)])],
                 ((0, 0), (0, LANES - 24))) for gp in range(2)], 1)
    return jnp.concatenate([q, z[:, _OPERM], k_s[:, _KPERM], v_s, k_w[:, _KPERM], v_w, k_c, v_c, gl], 1).astype(BF16)


def _phi_weights(w1, w2, rope_layout):
    h = w1.shape[1]
    w1 = w1.reshape(2, CMP_STRIDE, HEAD_DIM, h)
    eye = jnp.eye(2, dtype=w1.dtype)
    w1a, w1b = (jnp.einsum("ldj,gh->lgdhj", w1[t], eye).reshape(CMP_STRIDE * 2 * HEAD_DIM, 2 * h) for t in range(2))
    if rope_layout:
        w2 = jnp.einsum("jpi,gh->gjphi", w2.reshape(h, 2, 32), eye).reshape(2 * h, LANES)
    else:
        w2 = jnp.einsum("jd,gh->gjhd", w2, eye).reshape(2 * h, LANES)
    return w1a.astype(BF16), w1b.astype(BF16), w2.astype(BF16)


def _overlap_matrix_t(n_cmp, n_cmp_pad, n_blk):
    cstart = np.arange(n_cmp_pad) * CMP_STRIDE
    j = np.arange(LANES)
    ov = (cstart[None, :] < (j[:, None] + 1) * SEL_BLOCK) & (cstart[None, :] + CMP_BLOCK > j[:, None] * SEL_BLOCK)
    ov &= (np.arange(n_cmp_pad)[None, :] < n_cmp) & (j[:, None] < n_blk)
    return jnp.asarray(ov.astype(np.float32), dtype=BF16)


def _layer_swa(x, tabs64, w_in, sinks):
    q, z, k, vt = _proj_gqa(x, _swa_weights(w_in), *tabs64, n_kv=1, with_cmp=False)
    o = _flash_gqa(q, k, vt, mode="band", window=SWA_WINDOW, sinks=sinks.astype(F32), nsub=8)
    return [o], z, _OPERM


def _layer_mla(x, tabs32, w_in, q_norm, kv_norm, w_uq, w_ukv):
    lat = MLA_Q_LORA + MLA_KV_LORA
    w1 = jnp.concatenate([w_in[:, :lat], w_in[:, lat:lat + MLA_ROPE][:, _KRPERM], w_in[:, lat + MLA_ROPE:]], 1)
    wkv = jnp.concatenate([w_ukv[:, _KNPERM], w_ukv[:, _VPERM]], 1)
    q, k, vt, kr, z = _proj_mla(x, w1.astype(BF16), q_norm[None, :], kv_norm[None, :],
                                w_uq[:, _UQPERM].astype(BF16), wkv.astype(BF16), *tabs32)
    return [_flash_mla(q, k, vt, kr)], z, None


def _layer_nsa(x, tabs64, positions, w_in, cmp_pos, phi_k1, phi_k2, phi_v1, phi_v2):
    B, S, _ = x.shape
    q, z, ks, vst, kw, vwt, kraw, vraw, gate = _proj_gqa(x, _nsa_weights(w_in), *tabs64, n_kv=2, with_cmp=True)
    n_chunk = S // CMP_STRIDE
    n_cmp = (S - CMP_BLOCK) // CMP_STRIDE + 1
    pos2 = jnp.tile(cmp_pos.reshape(2, CMP_STRIDE, 1, HEAD_DIM), (1, 1, 2, 1)).reshape(2, 1, CMP_STRIDE * LANES)
    end = jnp.minimum(jnp.arange(n_chunk) * CMP_STRIDE + CMP_BLOCK - 1, S - 1)
    cos_c, sin_c = _rope_tables(positions[:, end].reshape(-1), HEAD_DIM // 2)
    kc, vct = _compress(kraw, vraw, pos2[0], pos2[1], _phi_weights(phi_k1, phi_k2, True),
                        _phi_weights(phi_v1, phi_v2, False),
                        cos_c.reshape(B, n_chunk, LANES), sin_c.reshape(B, n_chunk, LANES))
    ovt = _overlap_matrix_t(n_cmp, n_chunk, S // SEL_BLOCK)
    o_cmp, sel = _cmp_select(q, kc, vct, ovt, gate)
    o_slc = _flash_gqa(q, ks, vst, mode="causal", sel=sel, gate=gate, gate_base=8, tq=256)
    o_win = _flash_gqa(q, kw, vwt, mode="band", window=NSA_WINDOW, gate=gate, gate_base=16, nsub=4)
    return [o_cmp, o_slc, o_win], z, _OPERM


def kernel(x, p, positions, w_out, ln_g, ln_b, pe_gate, pe_proj, l0_w_in, l0_sinks, l1_w_in, l1_q_norm, l1_kv_norm, l1_w_uq, l1_w_ukv, l2_w_in, l2_cmp_pos, l2_phi_k1, l2_phi_k2, l2_phi_v1, l2_phi_v2, l3_w_in, l3_sinks):
    B, S, _ = x.shape
    pos_flat = positions.reshape(-1)
    tabs64 = [t.reshape(B, S, LANES) for t in _rope_tables(pos_flat, HEAD_DIM // 2)]
    tabs32 = [t.reshape(B, S, LANES) for t in _rope_tables(pos_flat, MLA_ROPE // 2)]
    layers = (
        lambda h: _layer_swa(h, tabs64, l0_w_in, l0_sinks),
        lambda h: _layer_mla(h, tabs32, l1_w_in, l1_q_norm, l1_kv_norm, l1_w_uq, l1_w_ukv),
        lambda h: _layer_nsa(h, tabs64, positions, l2_w_in, l2_cmp_pos, l2_phi_k1, l2_phi_k2, l2_phi_v1, l2_phi_v2),
        lambda h: _layer_swa(h, tabs64, l3_w_in, l3_sinks),
    )
    for i, layer in enumerate(layers):
        o_list, z, operm = layer(x)
        wo = w_out[i] if operm is None else w_out[i][operm, :]
        x = _out_block(o_list, z, x, p, i, wo.astype(BF16), ln_g[i][None, :], ln_b[i][None, :],
                       pe_gate[i].astype(BF16), pe_proj[i].astype(BF16))
    return x
```
